```python
import math
import jax, jax.numpy as jnp
from jax import lax
import numpy as np

D_MODEL = 1024
BATCH = 8
SEQ = 2048
DEPTH = 1

DN_HEADS = 8
DN_HEAD_DIM = 128
DN_W = DN_HEADS * DN_HEAD_DIM
DN_CONV = 4
DN_CHUNK = 64
NSA_HEADS = 16
NSA_GROUPS = 4
NSA_REP = NSA_HEADS // NSA_GROUPS
NSA_HEAD_DIM = 64
NSA_W = NSA_HEADS * NSA_HEAD_DIM
NSA_KV_W = NSA_GROUPS * NSA_HEAD_DIM
CMP_LEN = 32
CMP_STRIDE = 16
CMP_HIDDEN = 2 * NSA_HEAD_DIM
SLC_LEN = 64
SLC_TOPK = 8
WINDOW = 256
ATT_QBLK = 128
SLC_QBLK = 64
D_FF = 4 * D_MODEL
IN_WIDTH = 4 * DN_W + 2 * DN_HEADS + NSA_W + 6 * NSA_KV_W + 3 * NSA_HEADS + 2 * D_MODEL
EPS = 1e-6
NEG = -1e30
FORCE = 1e9

kernel_name = "hybrid_gdn_nsa_parallel_block"


def _in_split_points():
    sizes = [3 * DN_W, DN_W, DN_HEADS, DN_HEADS,
             NSA_W, NSA_KV_W, NSA_KV_W, NSA_KV_W, NSA_KV_W, NSA_KV_W, NSA_KV_W, 3 * NSA_HEADS,
             D_MODEL, D_MODEL]
    return [int(v) for v in np.cumsum(sizes)[:-1]]


def rmsnorm(x, gain):
    xf = x.astype(jnp.float32)
    y = xf * lax.rsqrt(jnp.mean(xf * xf, axis=-1, keepdims=True) + EPS)
    return (y * gain.astype(jnp.float32)).astype(x.dtype)


def l2norm(x):
    xf = x.astype(jnp.float32)
    return (xf * lax.rsqrt(jnp.sum(xf * xf, axis=-1, keepdims=True) + EPS)).astype(x.dtype)


def alibi_slopes(n_heads):
    return jnp.asarray(2.0 ** (-8.0 * np.arange(1, n_heads + 1) / n_heads), jnp.float32)


def causal_depthwise_conv(x, w):
    k, c = w.shape
    return lax.conv_general_dilated(x, w[:, None, :].astype(x.dtype), window_strides=(1,),
                                    padding=[(k - 1, 0)], dimension_numbers=('NWC', 'WIO', 'NWC'),
                                    feature_group_count=c)


def chunk_gated_delta_rule(q, k, v, g, beta):
    B, S, H, dk = q.shape
    dv = v.shape[-1]
    C = DN_CHUNK
    N = S // C
    f32 = jnp.float32

    def chunks(t):
        t = t.astype(f32).reshape((B, N, C, H) + t.shape[3:])
        return jnp.moveaxis(t, (1, 3), (0, 2))

    q = chunks(q) * (dk ** -0.5)
    k = chunks(k)
    v = chunks(v)
    beta = chunks(beta)
    gc = jnp.cumsum(chunks(g), axis=-1)
    kb = k * beta[..., None]
    vb = v * beta[..., None]
    causal = jnp.tril(jnp.ones((C, C), bool))
    strict = jnp.tril(jnp.ones((C, C), bool), -1)
    decay = jnp.exp(jnp.where(causal, gc[..., :, None] - gc[..., None, :], -jnp.inf))
    a = jnp.where(strict, jnp.einsum('nbhid,nbhjd->nbhij', kb, k) * decay, 0.0)
    t_mat = a + jnp.eye(C, dtype=f32)
    u = lax.linalg.triangular_solve(t_mat, vb, left_side=True, lower=True)
    w = lax.linalg.triangular_solve(t_mat, kb * jnp.exp(gc)[..., None], left_side=True, lower=True)
    qk = jnp.einsum('nbhid,nbhjd->nbhij', q, k) * decay
    q_dec = q * jnp.exp(gc)[..., None]
    k_end = k * jnp.exp(gc[..., -1:] - gc)[..., None]
    g_last = jnp.exp(gc[..., -1])

    def step(state, inp):
        qd, ke, u_c, w_c, qk_c, gl = inp
        v_new = u_c - jnp.einsum('bhcd,bhde->bhce', w_c, state)
        o = jnp.einsum('bhcd,bhde->bhce', qd, state) + jnp.einsum('bhij,bhje->bhie', qk_c, v_new)
        state = state * gl[..., None, None] + jnp.einsum('bhcd,bhce->bhde', ke, v_new)
        return state, o

    s0 = jnp.zeros((B, H, dk, dv), f32)
    _, o = lax.scan(step, s0, (q_dec, k_end, u, w, qk, g_last))
    return jnp.moveaxis(o, (0, 2), (1, 3)).reshape(B, S, H, dv)


def gated_deltanet(qkv, z, b_logit, a_logit, conv_w, a_log, dt_bias, out_norm):
    B, S, _ = qkv.shape
    f32 = jnp.float32
    qkv = jax.nn.silu(causal_depthwise_conv(qkv, conv_w))
    q, k, v = jnp.split(qkv, 3, axis=-1)
    q = l2norm(q.reshape(B, S, DN_HEADS, DN_HEAD_DIM))
    k = l2norm(k.reshape(B, S, DN_HEADS, DN_HEAD_DIM))
    v = v.reshape(B, S, DN_HEADS, DN_HEAD_DIM)
    beta = jax.nn.sigmoid(b_logit.astype(f32))
    g = -jnp.exp(a_log.astype(f32)) * jax.nn.softplus(a_logit.astype(f32) + dt_bias.astype(f32))
    o = chunk_gated_delta_rule(q, k, v, g, beta)
    o = rmsnorm(o, out_norm) * jax.nn.silu(z.reshape(B, S, DN_HEADS, DN_HEAD_DIM).astype(f32))
    return o.reshape(B, S, DN_W).astype(qkv.dtype)


def cmp_to_slc_overlap(n_cmp, n_sel):
    c0 = np.arange(n_cmp)[:, None] * CMP_STRIDE
    j0 = np.arange(n_sel)[None, :] * SLC_LEN
    ov = np.clip(np.minimum(c0 + CMP_LEN, j0 + SLC_LEN) - np.maximum(c0, j0), 0, None) / CMP_LEN
    return jnp.asarray(ov, jnp.float32)


def compress_blocks(t, pos_emb, w1, w2, idx):
    B, _, G, d = t.shape
    n_cmp = idx.shape[0]
    blk = t[:, idx] + pos_emb[:, None, :].astype(t.dtype)
    blk = jnp.moveaxis(blk, 3, 2).reshape(B, n_cmp, G, CMP_LEN * d)
    return jax.nn.gelu(blk @ w1) @ w2


def window_branch(q, k, v, slopes):
    B, S, G, R, _ = q.shape
    span = WINDOW + ATT_QBLK
    kp = jnp.pad(k, ((0, 0), (WINDOW, 0), (0, 0), (0, 0)))
    vp = jnp.pad(v, ((0, 0), (WINDOW, 0), (0, 0), (0, 0)))

    def block(i):
        t0 = i * ATT_QBLK
        qb = lax.dynamic_slice_in_dim(q, t0, ATT_QBLK, axis=1)
        kb = lax.dynamic_slice_in_dim(kp, t0, span, axis=1)
        vb = lax.dynamic_slice_in_dim(vp, t0, span, axis=1)
        qpos = t0 + jnp.arange(ATT_QBLK)
        kpos = t0 - WINDOW + jnp.arange(span)
        dist = qpos[:, None] - kpos[None, :]
        mask = (dist >= 0) & (dist < WINDOW) & (kpos[None, :] >= 0)
        s = jnp.einsum('bqgrd,bkgd->bgrqk', qb, kb).astype(jnp.float32) - slopes[:, :, None, None] * dist
        p = jax.nn.softmax(jnp.where(mask, s, NEG), axis=-1).astype(vb.dtype)
        return jnp.einsum('bgrqk,bkge->bqgre', p, vb)

    o = lax.map(block, jnp.arange(S // ATT_QBLK))
    return jnp.moveaxis(o, 0, 1).reshape(B, S, G, R, -1)


def selection_branch(q, k, v, sel_idx, slopes):
    B, S, G, R, dk = q.shape
    dv = v.shape[-1]
    n_sel = S // SLC_LEN
    K = sel_idx.shape[-1]
    kblk = jnp.moveaxis(k.reshape(B, n_sel, SLC_LEN, G, dk), 3, 1).reshape(B, G, n_sel, SLC_LEN * dk)
    vblk = jnp.moveaxis(v.reshape(B, n_sel, SLC_LEN, G, dv), 3, 1).reshape(B, G, n_sel, SLC_LEN * dv)
    gather = jax.vmap(jax.vmap(lambda tab, ix: tab[ix]))

    def block(i):
        t0 = i * SLC_QBLK
        qb = lax.dynamic_slice_in_dim(q, t0, SLC_QBLK, axis=1)
        ib = lax.dynamic_slice_in_dim(sel_idx, t0, SLC_QBLK, axis=2)
        flat = ib.reshape(B, G, SLC_QBLK * K)
        kg = gather(kblk, flat).reshape(B, G, SLC_QBLK, K * SLC_LEN, dk)
        vg = gather(vblk, flat).reshape(B, G, SLC_QBLK, K * SLC_LEN, dv)
        kpos = (ib[..., None] * SLC_LEN + jnp.arange(SLC_LEN)).reshape(B, G, SLC_QBLK, K * SLC_LEN)
        qpos = t0 + jnp.arange(SLC_QBLK)
        dist = (qpos[:, None] - kpos)[:, :, None]
        s = jnp.einsum('bqgrd,bgqkd->bgrqk', qb, kg).astype(jnp.float32) - slopes[None, :, :, None, None] * dist
        p = jax.nn.softmax(jnp.where(dist >= 0, s, NEG), axis=-1).astype(vg.dtype)
        return jnp.einsum('bgrqk,bgqke->bqgre', p, vg)

    o = lax.map(block, jnp.arange(S // SLC_QBLK))
    return jnp.moveaxis(o, 0, 1).reshape(B, S, G, R, dv)


def nsa_mixer(q, k_c, v_c, k_s, v_s, k_w, v_w, gate_logits, q_norm, kn_cmp, kn_slc, kn_win,
              cmp_pos_k, cmp_w1_k, cmp_w2_k, cmp_pos_v, cmp_w1_v, cmp_w2_v):
    B, S, _ = q.shape
    G, R, d = NSA_GROUPS, NSA_REP, NSA_HEAD_DIM
    f32 = jnp.float32
    slopes = alibi_slopes(NSA_HEADS).reshape(G, R)
    pos = jnp.arange(S)
    q = (rmsnorm(q.reshape(B, S, NSA_HEADS, d), q_norm) * (d ** -0.5)).reshape(B, S, G, R, d)
    kv = lambda t: t.reshape(B, S, G, d)

    n_cmp = (S - CMP_LEN) // CMP_STRIDE + 1
    idx = np.arange(n_cmp)[:, None] * CMP_STRIDE + np.arange(CMP_LEN)[None, :]
    kc = rmsnorm(compress_blocks(kv(k_c), cmp_pos_k, cmp_w1_k, cmp_w2_k, idx), kn_cmp)
    vc = compress_blocks(kv(v_c), cmp_pos_v, cmp_w1_v, cmp_w2_v, idx)
    cmp_end = jnp.asarray(idx[:, -1])
    dist_c = pos[:, None] - cmp_end[None, :]
    s_c = jnp.einsum('bsgrd,bcgd->bgrsc', q, kc).astype(f32) - slopes[:, :, None, None] * dist_c
    any_valid = (pos >= CMP_LEN - 1).astype(f32)[:, None]
    p_c = jax.nn.softmax(jnp.where(dist_c >= 0, s_c, NEG), axis=-1) * any_valid
    o_cmp = jnp.einsum('bgrsc,bcgd->bsgrd', p_c.astype(vc.dtype), vc)

    n_sel = S // SLC_LEN
    imp = jnp.einsum('bgrsc,cj->bgsj', p_c, cmp_to_slc_overlap(n_cmp, n_sel))
    j = jnp.arange(n_sel)[None, :]
    blk_t = (pos // SLC_LEN)[:, None]
    imp = jnp.where(j <= blk_t, imp, NEG)
    imp = jnp.where((j == blk_t) | (j == 0), FORCE, imp)
    _, sel_idx = lax.top_k(imp, min(SLC_TOPK, n_sel))
    o_slc = selection_branch(q, rmsnorm(kv(k_s), kn_slc), kv(v_s), sel_idx, slopes)

    o_win = window_branch(q, rmsnorm(kv(k_w), kn_win), kv(v_w), slopes)

    gates = jax.nn.sigmoid(gate_logits.astype(f32)).reshape(B, S, G, R, 3)
    o = gates[..., 0:1] * o_cmp + gates[..., 1:2] * o_slc + gates[..., 2:3] * o_win
    return o.reshape(B, S, NSA_W).astype(q.dtype)


def setup_inputs(seed: int = 0) -> dict:
    key = jax.random.key(seed)
    ks = jax.random.split(key, 24)
    f32 = jnp.float32
    L = DEPTH

    def nrm(k, shape, scale):
        return jax.random.normal(k, shape, f32) * scale

    def gain(k, n):
        return 1.0 + 0.02 * jax.random.normal(k, (L, n), f32)

    dt = jnp.exp(jax.random.uniform(ks[4], (L, DN_HEADS), f32, math.log(1e-3), math.log(1e-1)))
    flat = CMP_LEN * NSA_HEAD_DIM
    return {
        "x": nrm(ks[0], (BATCH, SEQ, D_MODEL), 1.0),
        "norm_mix": gain(ks[1], D_MODEL),
        "w_in": nrm(ks[2], (L, D_MODEL, IN_WIDTH), D_MODEL ** -0.5),
        "dn_conv": nrm(ks[3], (L, DN_CONV, 3 * DN_W), DN_CONV ** -0.5),
        "dn_a_log": jnp.log(jax.random.uniform(ks[5], (L, DN_HEADS), f32, 1.0, 16.0)),
        "dn_dt_bias": dt + jnp.log(-jnp.expm1(-dt)),
        "dn_out_norm": gain(ks[6], DN_HEAD_DIM),
        "nsa_q_norm": gain(ks[7], NSA_HEAD_DIM),
        "nsa_k_norm_cmp": gain(ks[8], NSA_HEAD_DIM),
        "nsa_k_norm_slc": gain(ks[9], NSA_HEAD_DIM),
        "nsa_k_norm_win": gain(ks[10], NSA_HEAD_DIM),
        "cmp_pos_k": nrm(ks[11], (L, CMP_LEN, NSA_HEAD_DIM), 0.1),
        "cmp_w1_k": nrm(ks[12], (L, flat, CMP_HIDDEN), flat ** -0.5),
        "cmp_w2_k": nrm(ks[13], (L, CMP_HIDDEN, NSA_HEAD_DIM), CMP_HIDDEN ** -0.5),
        "cmp_pos_v": nrm(ks[14], (L, CMP_LEN, NSA_HEAD_DIM), 0.1),
        "cmp_w1_v": nrm(ks[15], (L, flat, CMP_HIDDEN), flat ** -0.5),
        "cmp_w2_v": nrm(ks[16], (L, CMP_HIDDEN, NSA_HEAD_DIM), CMP_HIDDEN ** -0.5),
        "w_proj_dn": nrm(ks[17], (L, DN_W, D_MODEL), DN_W ** -0.5),
        "w_proj_nsa": nrm(ks[18], (L, NSA_W, D_MODEL), NSA_W ** -0.5),
        "w_out": nrm(ks[19], (L, D_MODEL, D_MODEL), D_MODEL ** -0.5),
        "norm_mlp": gain(ks[20], D_MODEL),
        "w_up": nrm(ks[21], (L, D_MODEL, D_FF), D_MODEL ** -0.5),
        "w_down": nrm(ks[22], (L, D_FF, D_MODEL), D_FF ** -0.5),
    }


def reference(x, norm_mix, w_in, dn_conv, dn_a_log, dn_dt_bias, dn_out_norm,
              nsa_q_norm, nsa_k_norm_cmp, nsa_k_norm_slc, nsa_k_norm_win,
              cmp_pos_k, cmp_w1_k, cmp_w2_k, cmp_pos_v, cmp_w1_v, cmp_w2_v,
              w_proj_dn, w_proj_nsa, w_out, norm_mlp, w_up, w_down):
    splits = _in_split_points()
    for l in range(DEPTH):
        h = rmsnorm(x, norm_mix[l])
        (dn_qkv, dn_z, dn_b, dn_a, ns_q, ns_kc, ns_vc, ns_ks, ns_vs, ns_kw, ns_vw, ns_gate,
         gate_dn, gate_nsa) = jnp.split(h @ w_in[l], splits, axis=-1)
        o_dn = gated_deltanet(dn_qkv, dn_z, dn_b, dn_a, dn_conv[l], dn_a_log[l], dn_dt_bias[l], dn_out_norm[l])
        o_ns = nsa_mixer(ns_q, ns_kc, ns_vc, ns_ks, ns_vs, ns_kw, ns_vw, ns_gate,
                         nsa_q_norm[l], nsa_k_norm_cmp[l], nsa_k_norm_slc[l], nsa_k_norm_win[l],
                         cmp_pos_k[l], cmp_w1_k[l], cmp_w2_k[l], cmp_pos_v[l], cmp_w1_v[l], cmp_w2_v[l])
        mix = jax.nn.sigmoid(gate_dn) * (o_dn @ w_proj_dn[l]) + jax.nn.sigmoid(gate_nsa) * (o_ns @ w_proj_nsa[l])
        x = x + mix @ w_out[l]
        h2 = rmsnorm(x, norm_mlp[l])
        x = x + jnp.square(jax.nn.relu(h2 @ w_up[l])) @ w_down[l]
    return x
```

```python
import functools
import math

import numpy as np
import jax
import jax.numpy as jnp
from jax import lax
from jax.experimental import pallas as pl
from jax.experimental.pallas import tpu as pltpu

F32 = jnp.float32
BF16 = jnp.bfloat16
HIGHEST = lax.Precision.HIGHEST

D_MODEL = 1024
DN_HEADS = 8
DN_HEAD_DIM = 128
DN_W = DN_HEADS * DN_HEAD_DIM
DN_CONV = 4
DN_CHUNK = 64
NSA_HEADS = 16
NSA_GROUPS = 4
NSA_REP = NSA_HEADS // NSA_GROUPS
NSA_HEAD_DIM = 64
NSA_W = NSA_HEADS * NSA_HEAD_DIM
NSA_KV_W = NSA_GROUPS * NSA_HEAD_DIM
CMP_LEN = 32
CMP_STRIDE = 16
CMP_HIDDEN = 2 * NSA_HEAD_DIM
SLC_LEN = 64
SLC_TOPK = 8
WINDOW = 256
D_FF = 4 * D_MODEL
EPS = 1e-6
NEG = -1e30
FORCE = 1e9
LANES = 128

COL_Q, COL_K, COL_V, COL_Z = 0, 8, 16, 24
COL_NSQ, COL_GDN, COL_GNS = 32, 40, 48
COL_KC, COL_VC, COL_KS, COL_VS, COL_KW, COL_VW = 56, 58, 60, 62, 64, 66
COL_SMALL = 68
P_WIDTH = (COL_SMALL + 1) * LANES
SMALL_B, SMALL_A, SMALL_GATE = 0, DN_HEADS, 2 * DN_HEADS

VMEM_LIMIT = 48 * 1024 * 1024


def _cparams(n_axes):
    return pltpu.CompilerParams(dimension_semantics=("arbitrary",) * n_axes,
                                vmem_limit_bytes=VMEM_LIMIT)


def _hdot(a, b):
    return jnp.dot(a, b, precision=HIGHEST, preferred_element_type=F32)


def _bdot(a, b):
    return jnp.dot(a.astype(BF16), b.astype(BF16), preferred_element_type=F32)


def _dot_nt(a, b, precision=None):
    return lax.dot_general(a, b, (((1,), (1,)), ((), ())), precision=precision,
                           preferred_element_type=F32)


def _dot_tn(a, b, precision=None):
    return lax.dot_general(a, b, (((0,), (0,)), ((), ())), precision=precision,
                           preferred_element_type=F32)


def _rms(x, gain):
    return x * lax.rsqrt(jnp.mean(x * x, axis=-1, keepdims=True) + EPS) * gain


def _inproj_kernel(x_ref, g_ref, w_ref, o_ref, h_ref):
    @pl.when(pl.program_id(1) == 0)
    def _():
        h_ref[...] = _rms(x_ref[...], g_ref[...]).astype(BF16)

    o_ref[...] = jnp.dot(h_ref[...], w_ref[...], preferred_element_type=F32)


def _in_proj(x2, gain, w_perm, tm=512, tn=2944):
    t = x2.shape[0]
    return pl.pallas_call(
        _inproj_kernel,
        grid=(t // tm, P_WIDTH // tn),
        in_specs=[pl.BlockSpec((tm, D_MODEL), lambda i, j: (i, 0)),
                  pl.BlockSpec((1, D_MODEL), lambda i, j: (0, 0)),
                  pl.BlockSpec((D_MODEL, tn), lambda i, j: (0, j))],
        out_specs=pl.BlockSpec((tm, tn), lambda i, j: (i, j)),
        out_shape=jax.ShapeDtypeStruct((t, P_WIDTH), F32),
        scratch_shapes=[pltpu.VMEM((tm, D_MODEL), BF16)],
        compiler_params=_cparams(2),
        name="in_proj",
    )(x2, gain, w_perm)


def _conv_silu(x, w):
    rows = lax.broadcasted_iota(jnp.int32, x.shape, 0)
    y = x * w[DN_CONV - 1:DN_CONV, :]
    for j in range(DN_CONV - 1):
        sh = DN_CONV - 1 - j
        xs = jnp.where(rows >= sh, pltpu.roll(x, sh, axis=0), 0.0)
        y = y + xs * w[j:j + 1, :]
    return y * jax.nn.sigmoid(y)


def _gdn_kernel(alog_ref, dtb_ref, q_ref, k_ref, v_ref, z_ref, sm_ref, cq_ref, ck_ref, cv_ref,
                onorm_ref, o_ref, qs, ks, vs, gs, bs, os_):
    h = pl.program_id(1)
    s_len = q_ref.shape[0]
    c = DN_CHUNK

    q = _conv_silu(q_ref[...], cq_ref[...])
    k = _conv_silu(k_ref[...], ck_ref[...])
    v = _conv_silu(v_ref[...], cv_ref[...])
    qs[...] = q * lax.rsqrt(jnp.sum(q * q, axis=-1, keepdims=True) + EPS) * (DN_HEAD_DIM ** -0.5)
    ks[...] = k * lax.rsqrt(jnp.sum(k * k, axis=-1, keepdims=True) + EPS)
    vs[...] = v

    sm = sm_ref[...]
    kk = lax.broadcasted_iota(jnp.int32, (LANES, LANES), 0)
    b_logit = _hdot(sm, (kk == SMALL_B + h).astype(F32))
    a_logit = _hdot(sm, (kk == SMALL_A + h).astype(F32))
    neg_a = -jnp.exp(jnp.full((1, LANES), alog_ref[0, h], F32))
    dtb = jnp.full((1, LANES), dtb_ref[0, h], F32)
    bs[...] = jax.nn.sigmoid(b_logit)
    gs[...] = neg_a * jax.nn.softplus(a_logit + dtb)

    ri = lax.broadcasted_iota(jnp.int32, (c, c), 0)
    ci = lax.broadcasted_iota(jnp.int32, (c, c), 1)
    causal = ri >= ci
    strict = ri > ci
    eye = (ri == ci).astype(F32)
    ltri = causal.astype(F32)
    ones = jnp.ones((c, c), F32)

    def chunk(ic, state):
        r = pl.ds(pl.multiple_of(ic * c, c), c)
        qc, kc, vc, bc = qs[r, :], ks[r, :], vs[r, :], bs[r, :]
        gcum = _hdot(ltri, gs[r, :])
        gcol = gcum[:, :c]
        grow = _hdot(ones, eye * gcol)
        decay = jnp.exp(jnp.where(causal, gcol - grow, -jnp.inf))
        kb = kc * bc
        vb = vc * bc
        a = jnp.where(strict, _dot_nt(kb, kc, HIGHEST) * decay, 0.0)
        tinv = eye - a
        pw = a
        for _ in range(int(math.log2(c)) - 1):
            pw = _hdot(pw, pw)
            tinv = tinv + _hdot(tinv, pw)
        eg = jnp.exp(gcum)
        u = _hdot(tinv, vb)
        w = _hdot(tinv, kb * eg)
        qk = _dot_nt(qc, kc, HIGHEST) * decay
        glast = gcum[c - 1:c, :]
        k_end = kc * jnp.exp(glast - gcum)
        v_new = u - _hdot(w, state)
        o = _hdot(qc * eg, state) + _hdot(qk, v_new)
        os_[r, :] = o
        return state * jnp.exp(glast) + _dot_tn(k_end, v_new, HIGHEST)

    lax.fori_loop(0, s_len // c, chunk, jnp.zeros((DN_HEAD_DIM, DN_HEAD_DIM), F32))

    z = z_ref[...]
    o_ref[...] = _rms(os_[...], onorm_ref[...]) * (z * jax.nn.sigmoid(z))


def _gdn(p, conv_w, a_log, dt_bias, out_norm, batch, seq):
    row_blk = lambda col0: pl.BlockSpec((seq, LANES), lambda b, h, col0=col0: (b, col0 + h))
    conv_blk = lambda col0: pl.BlockSpec((DN_CONV, LANES), lambda b, h, col0=col0: (0, col0 + h))
    smem = pl.BlockSpec(memory_space=pltpu.SMEM)
    return pl.pallas_call(
        _gdn_kernel,
        grid=(batch, DN_HEADS),
        in_specs=[smem, smem,
                  row_blk(COL_Q), row_blk(COL_K), row_blk(COL_V), row_blk(COL_Z),
                  pl.BlockSpec((seq, LANES), lambda b, h: (b, COL_SMALL)),
                  conv_blk(COL_Q), conv_blk(COL_K), conv_blk(COL_V),
                  pl.BlockSpec((1, LANES), lambda b, h: (0, 0))],
        out_specs=pl.BlockSpec((seq, LANES), lambda b, h: (b, h)),
        out_shape=jax.ShapeDtypeStruct((batch * seq, DN_W), F32),
        scratch_shapes=[pltpu.VMEM((seq, LANES), F32) for _ in range(6)],
        compiler_params=_cparams(2),
        name="gdn",
    )(a_log, dt_bias, p, p, p, p, p, conv_w, conv_w, conv_w, out_norm)


def _gelu_tanh(x):
    return 0.5 * x * (1.0 + jnp.tanh(math.sqrt(2.0 / math.pi) * (x + 0.044715 * (x * x * x))))


def _compress_one(t2, pos, w1, w2):
    n = t2.shape[0]
    half = t2.shape[1]
    y1 = _bdot(t2 + pos[:, :half], w1[:half, :])
    y2 = _bdot(t2 + pos[:, half:], w1[half:, :])
    hid = _gelu_tanh(y1 + pltpu.roll(y2, n - 1, axis=0))
    return _bdot(hid, w2)


def _compress_kernel(tk_ref, tv_ref, pk_ref, pv_ref, w1k_ref, w2k_ref, w1v_ref, w2v_ref, kn_ref,
                     kc_ref, vc_ref):
    kc = _compress_one(tk_ref[0, 0], pk_ref[...], w1k_ref[...], w2k_ref[...])
    kc_ref[0, 0] = _rms(kc, kn_ref[...])
    vc_ref[0, 0] = _compress_one(tv_ref[0, 0], pv_ref[...], w1v_ref[...], w2v_ref[...])


def _compress(t2k, t2v, pos_k, pos_v, w1k, w2k, w1v, w2v, kn_cmp):
    batch, groups, n, width = t2k.shape
    d = NSA_HEAD_DIM
    tok = pl.BlockSpec((1, 1, n, width), lambda b, g: (b, g, 0, 0))
    full = lambda a: pl.BlockSpec(a.shape, lambda b, g: (0,) * a.ndim)
    out = pl.BlockSpec((1, 1, n, d), lambda b, g: (b, g, 0, 0))
    return pl.pallas_call(
        _compress_kernel,
        grid=(batch, groups),
        in_specs=[tok, tok, full(pos_k), full(pos_v), full(w1k), full(w2k), full(w1v), full(w2v),
                  full(kn_cmp)],
        out_specs=[out, out],
        out_shape=[jax.ShapeDtypeStruct((batch, groups, n, d), F32)] * 2,
        compiler_params=_cparams(2),
        name="nsa_compress",
    )(t2k, t2v, pos_k, pos_v, w1k, w2k, w1v, w2v, kn_cmp)


def _softmax_rows(s):
    e = jnp.exp(s - jnp.max(s, axis=-1, keepdims=True))
    return e, jnp.sum(e, axis=-1, keepdims=True)


def _nsa_kernel(q_ref, ks_ref, vs_ref, kw_ref, vw_ref, kc_ref, vc_ref, sm_ref, qn_ref, kns_ref,
                knw_ref, ov_ref, blk_ref, o_ref, ksn, vsb, kwn, vwb, *, tq):
    it = pl.program_id(1)
    s_len = ks_ref.shape[0]
    d = NSA_HEAD_DIM
    rep = NSA_REP
    rows = rep * tq
    span = WINDOW + tq

    @pl.when(it == 0)
    def _():
        for g in range(NSA_GROUPS):
            sl = slice(g * d, (g + 1) * d)
            ksn[g] = _rms(ks_ref[:, sl], kns_ref[...]).astype(BF16)
            kwn[g] = _rms(kw_ref[:, sl], knw_ref[...]).astype(BF16)
            vsb[g] = vs_ref[:, sl].astype(BF16)
            vwb[g] = vw_ref[:, sl].astype(BF16)

    t0 = it * tq
    tpos = t0 + (lax.broadcasted_iota(jnp.int32, (rows, 1), 0) & (tq - 1))
    tpos_q = t0 + lax.broadcasted_iota(jnp.int32, (tq, 1), 0)
    lane = lax.broadcasted_iota(jnp.int32, (tq, LANES), 1)
    lane_f = lane.astype(F32)

    cend = CMP_STRIDE * lax.broadcasted_iota(jnp.int32, (1, LANES), 1) + (CMP_LEN - 1)
    dist_c = tpos - cend
    any_valid = (tpos >= CMP_LEN - 1).astype(F32)
    kpos = lax.broadcasted_iota(jnp.int32, (1, s_len), 1)
    dist_s = (tpos - kpos).astype(F32)
    causal_q = kpos <= tpos_q
    w0 = pl.multiple_of(jnp.maximum(t0 - WINDOW, 0), tq)
    dist_w = tpos - (w0 + lax.broadcasted_iota(jnp.int32, (1, span), 1))
    in_win = (dist_w >= 0) & (dist_w < WINDOW)

    blk_t = lax.shift_right_logical(tpos_q, int(math.log2(SLC_LEN)))
    n_sel = s_len // SLC_LEN
    gates = jax.nn.sigmoid(sm_ref[...])

    for g in range(NSA_GROUPS):
        slope = jnp.concatenate(
            [jnp.full((tq, 1), 2.0 ** (-8.0 * (g * rep + r + 1) / NSA_HEADS), F32) for r in range(rep)],
            axis=0)
        qh = [_rms(q_ref[:, (g * rep + r) * d:(g * rep + r + 1) * d], qn_ref[...]) * (d ** -0.5)
              for r in range(rep)]
        q = jnp.concatenate(qh, axis=0).astype(BF16)

        s = _dot_nt(q, kc_ref[0, g].astype(BF16)) - slope * dist_c.astype(F32)
        e, l = _softmax_rows(jnp.where(dist_c >= 0, s, NEG))
        p_c = e / l * any_valid
        o_cmp = _bdot(p_c, vc_ref[0, g])
        p_sum = p_c[0:tq]
        for r in range(1, rep):
            p_sum = p_sum + p_c[r * tq:(r + 1) * tq]

        imp = _hdot(p_sum, ov_ref[...])
        imp = jnp.where(lane <= blk_t, imp, NEG)
        imp = jnp.where((lane == blk_t) | (lane == 0), FORCE, imp)
        imp = jnp.where(lane < n_sel, imp, -3e38)
        sel = jnp.zeros((tq, LANES), F32)
        for _ in range(min(SLC_TOPK, n_sel)):
            m = jnp.max(imp, axis=-1, keepdims=True)
            idx = jnp.min(jnp.where(imp == m, lane_f, 1e9), axis=-1, keepdims=True)
            hit = lane_f == idx
            sel = jnp.where(hit, 1.0, sel)
            imp = jnp.where(hit, -3e38, imp)

        picked = jnp.dot(sel.astype(BF16), blk_ref[...], preferred_element_type=F32)
        bias = jnp.where((picked > 0.5) & causal_q, 0.0, NEG)
        s = _dot_nt(q, ksn[g]) - slope * dist_s + jnp.concatenate([bias] * rep, axis=0)
        e, l = _softmax_rows(s)
        o_slc = _bdot(e, vsb[g]) / l

        s = _dot_nt(q, kwn[g, pl.ds(w0, span), :]) - slope * dist_w.astype(F32)
        e, l = _softmax_rows(jnp.where(in_win, s, NEG))
        o_win = _bdot(e, vwb[g, pl.ds(w0, span), :]) / l

        for r in range(rep):
            hd = g * rep + r
            rs = slice(r * tq, (r + 1) * tq)
            c0 = SMALL_GATE + 3 * hd
            o = (gates[:, c0:c0 + 1] * o_cmp[rs] + gates[:, c0 + 1:c0 + 2] * o_slc[rs]
                 + gates[:, c0 + 2:c0 + 3] * o_win[rs])
            o_ref[:, hd * d:(hd + 1) * d] = o


def _nsa(p, kc, vc, q_norm, kn_slc, kn_win, ov, blk, batch, seq, tq=64):
    nt = seq // tq
    kv = lambda col: pl.BlockSpec((seq, NSA_KV_W), lambda b, i, col=col: (b, col // 2))
    full = lambda a: pl.BlockSpec(a.shape, lambda b, i: (0,) * a.ndim)
    cmp_blk = pl.BlockSpec((1,) + kc.shape[1:], lambda b, i: (b, 0, 0, 0))
    return pl.pallas_call(
        functools.partial(_nsa_kernel, tq=tq),
        grid=(batch, nt),
        in_specs=[pl.BlockSpec((tq, NSA_W), lambda b, i: (b * nt + i, COL_NSQ // 8)),
                  kv(COL_KS), kv(COL_VS), kv(COL_KW), kv(COL_VW),
                  cmp_blk, cmp_blk,
                  pl.BlockSpec((tq, LANES), lambda b, i: (b * nt + i, COL_SMALL)),
                  full(q_norm), full(kn_slc), full(kn_win), full(ov), full(blk)],
        out_specs=pl.BlockSpec((tq, NSA_W), lambda b, i: (b * nt + i, 0)),
        out_shape=jax.ShapeDtypeStruct((batch * seq, NSA_W), F32),
        scratch_shapes=[pltpu.VMEM((NSA_GROUPS, seq, NSA_HEAD_DIM), BF16) for _ in range(4)],
        compiler_params=_cparams(2),
        name="nsa_attention",
    )(p, p, p, p, p, kc, vc, p, q_norm, kn_slc, kn_win, ov, blk)


def _mix_kernel(x_ref, odn_ref, ons_ref, gdn_ref, gns_ref, wdn_ref, wns_ref, wout_ref, o_ref):
    y_dn = jnp.dot(odn_ref[...].astype(BF16), wdn_ref[...], preferred_element_type=F32)
    y_ns = jnp.dot(ons_ref[...].astype(BF16), wns_ref[...], preferred_element_type=F32)
    mix = jax.nn.sigmoid(gdn_ref[...]) * y_dn + jax.nn.sigmoid(gns_ref[...]) * y_ns
    o_ref[...] = x_ref[...] + jnp.dot(mix.astype(BF16), wout_ref[...], preferred_element_type=F32)


def _mix(x2, o_dn, o_ns, p, w_dn, w_ns, w_out, tm=512):
    t = x2.shape[0]
    row = lambda col: pl.BlockSpec((tm, D_MODEL), lambda i, col=col: (i, col))
    wfull = pl.BlockSpec((D_MODEL, D_MODEL), lambda i: (0, 0))
    return pl.pallas_call(
        _mix_kernel,
        grid=(t // tm,),
        in_specs=[row(0), row(0), row(0), row(COL_GDN // 8), row(COL_GNS // 8), wfull, wfull, wfull],
        out_specs=row(0),
        out_shape=jax.ShapeDtypeStruct((t, D_MODEL), F32),
        compiler_params=_cparams(1),
        name="mix_out",
    )(x2, o_dn, o_ns, p, p, w_dn, w_ns, w_out)


def _ffn_kernel(x_ref, g_ref, wup_ref, wdown_ref, o_ref, h_ref, acc_ref):
    f = pl.program_id(1)

    @pl.when(f == 0)
    def _():
        h_ref[...] = _rms(x_ref[...], g_ref[...]).astype(BF16)
        acc_ref[...] = jnp.zeros_like(acc_ref)

    u = jnp.dot(h_ref[...], wup_ref[...], preferred_element_type=F32)
    u = jnp.square(jnp.maximum(u, 0.0)).astype(BF16)
    acc_ref[...] += jnp.dot(u, wdown_ref[...], preferred_element_type=F32)

    @pl.when(f == pl.num_programs(1) - 1)
    def _():
        o_ref[...] = x_ref[...] + acc_ref[...]


def _ffn(x2, gain, w_up, w_down, tm=512, tf=1024):
    t = x2.shape[0]
    return pl.pallas_call(
        _ffn_kernel,
        grid=(t // tm, D_FF // tf),
        in_specs=[pl.BlockSpec((tm, D_MODEL), lambda i, f: (i, 0)),
                  pl.BlockSpec((1, D_MODEL), lambda i, f: (0, 0)),
                  pl.BlockSpec((D_MODEL, tf), lambda i, f: (0, f)),
                  pl.BlockSpec((tf, D_MODEL), lambda i, f: (f, 0))],
        out_specs=pl.BlockSpec((tm, D_MODEL), lambda i, f: (i, 0)),
        out_shape=jax.ShapeDtypeStruct((t, D_MODEL), F32),
        scratch_shapes=[pltpu.VMEM((tm, D_MODEL), BF16), pltpu.VMEM((tm, D_MODEL), F32)],
        compiler_params=_cparams(2),
        name="ffn",
    )(x2, gain, w_up, w_down)


def _overlap_matrix(seq):
    n_cmp = (seq - CMP_LEN) // CMP_STRIDE + 1
    n_sel = seq // SLC_LEN
    c0 = np.arange(n_cmp)[:, None] * CMP_STRIDE
    j0 = np.arange(n_sel)[None, :] * SLC_LEN
    ov = np.clip(np.minimum(c0 + CMP_LEN, j0 + SLC_LEN) - np.maximum(c0, j0), 0, None) / CMP_LEN
    out = np.zeros((LANES, LANES), np.float32)
    out[:n_cmp, :n_sel] = ov
    return jnp.asarray(out)


def _block_indicator(seq):
    j = np.arange(LANES)[:, None]
    key = np.arange(seq)[None, :]
    return jnp.asarray((key // SLC_LEN == j).astype(np.float32), BF16)


def _regroup_w_in(w):
    o_b = 4 * DN_W
    o_q = o_b + 2 * DN_HEADS
    o_kv = o_q + NSA_W
    o_gate = o_kv + 6 * NSA_KV_W
    o_gdn = o_gate + 3 * NSA_HEADS
    pad = jnp.zeros((w.shape[0], LANES - 2 * DN_HEADS - 3 * NSA_HEADS), w.dtype)
    return jnp.concatenate(
        [w[:, :o_b], w[:, o_q:o_kv], w[:, o_gdn:], w[:, o_kv:o_gate], w[:, o_b:o_q],
         w[:, o_gate:o_gdn], pad], axis=1).astype(BF16)


def _blocks16(p, col, batch, seq):
    t = p[:, col * LANES:col * LANES + NSA_KV_W]
    t = t.reshape(batch, seq // CMP_STRIDE, CMP_STRIDE, NSA_GROUPS, NSA_HEAD_DIM)
    t = jnp.transpose(t, (0, 3, 1, 2, 4))
    return t.reshape(batch, NSA_GROUPS, seq // CMP_STRIDE, CMP_STRIDE * NSA_HEAD_DIM)


def kernel(x, norm_mix, w_in, dn_conv, dn_a_log, dn_dt_bias, dn_out_norm, nsa_q_norm, nsa_k_norm_cmp,
           nsa_k_norm_slc, nsa_k_norm_win, cmp_pos_k, cmp_w1_k, cmp_w2_k, cmp_pos_v, cmp_w1_v, cmp_w2_v,
           w_proj_dn, w_proj_nsa, w_out, norm_mlp, w_up, w_down):
    batch, seq, _ = x.shape
    assert seq // CMP_STRIDE == LANES and seq % DN_CHUNK == 0
    ov = _overlap_matrix(seq)
    blk = _block_indicator(seq)
    x2 = x.reshape(batch * seq, D_MODEL)
    for l in range(w_in.shape[0]):
        p = _in_proj(x2, norm_mix[l][None], _regroup_w_in(w_in[l]))
        o_dn = _gdn(p, dn_conv[l], dn_a_log[l][None], dn_dt_bias[l][None], dn_out_norm[l][None],
                    batch, seq)
        flat = lambda a: a.reshape(1, -1)
        kc, vc = _compress(_blocks16(p, COL_KC, batch, seq), _blocks16(p, COL_VC, batch, seq),
                           flat(cmp_pos_k[l]), flat(cmp_pos_v[l]),
                           cmp_w1_k[l].astype(BF16), cmp_w2_k[l].astype(BF16),
                           cmp_w1_v[l].astype(BF16), cmp_w2_v[l].astype(BF16),
                           nsa_k_norm_cmp[l][None])
        o_ns = _nsa(p, kc, vc, nsa_q_norm[l][None], nsa_k_norm_slc[l][None], nsa_k_norm_win[l][None],
                    ov, blk, batch, seq)
        x2 = _mix(x2, o_dn, o_ns, p, w_proj_dn[l].astype(BF16), w_proj_nsa[l].astype(BF16),
                  w_out[l].astype(BF16))
        x2 = _ffn(x2, norm_mlp[l][None], w_up[l].astype(BF16), w_down[l].astype(BF16))
    return x2.reshape(batch, seq, D_MODEL)
```

```python
import functools
import math

import numpy as np
import jax
import jax.numpy as jnp
from jax import lax
from jax.experimental import pallas as pl
from jax.experimental.pallas import tpu as pltpu

F32 = jnp.float32
BF16 = jnp.bfloat16

D_MODEL = 1024
DN_HEADS = 8
DN_HEAD_DIM = 128
DN_W = DN_HEADS * DN_HEAD_DIM
DN_CONV = 4
NSA_HEADS = 16
NSA_GROUPS = 4
NSA_REP = NSA_HEADS // NSA_GROUPS
NSA_HEAD_DIM = 64
NSA_W = NSA_HEADS * NSA_HEAD_DIM
NSA_KV_W = NSA_GROUPS * NSA_HEAD_DIM
CMP_LEN = 32
CMP_STRIDE = 16
CMP_HIDDEN = 2 * NSA_HEAD_DIM
SLC_LEN = 64
SLC_TOPK = 8
WINDOW = 256
D_FF = 4 * D_MODEL
EPS = 1e-6
NEG = -1e30
FORCE = 1e9
LANES = 128

DN_CHUNK = LANES
DN_INTRA_GROUP = 4
ATT_TILE = WINDOW

COL_Q, COL_K, COL_V, COL_Z = 0, 8, 16, 24
COL_NSQ, COL_GDN, COL_GNS = 32, 40, 48
COL_KC, COL_VC, COL_KS, COL_VS, COL_KW, COL_VW = 56, 58, 60, 62, 64, 66
COL_SMALL = 68
P_WIDTH = (COL_SMALL + 1) * LANES
SMALL_B, SMALL_A, SMALL_GATE = 0, DN_HEADS, 2 * DN_HEADS

X_SLOPE = SLC_TOPK * 4

VMEM_LIMIT = 48 * 1024 * 1024


def _cparams(n_axes):
    return pltpu.CompilerParams(dimension_semantics=("arbitrary",) * n_axes,
                                vmem_limit_bytes=VMEM_LIMIT)


def _bdot(a, b):
    return jnp.dot(a.astype(BF16), b.astype(BF16), preferred_element_type=F32)


def _dot_nt(a, b):
    return lax.dot_general(a.astype(BF16), b.astype(BF16), (((1,), (1,)), ((), ())),
                           preferred_element_type=F32)


def _dot_tn(a, b):
    return lax.dot_general(a.astype(BF16), b.astype(BF16), (((0,), (0,)), ((), ())),
                           preferred_element_type=F32)


def _split3(a):
    a1 = a.astype(BF16)
    r = a - a1.astype(F32)
    a2 = r.astype(BF16)
    a3 = (r - a2.astype(F32)).astype(BF16)
    return a1, a2, a3


def _hi_lo(a):
    hi = a.astype(BF16)
    return hi, (a - hi.astype(F32)).astype(BF16)


def _dot_hl(x, y):
    (xh, xl), (yh, yl) = x, y
    return (jnp.dot(jnp.concatenate([xh, xl], axis=1), jnp.concatenate([yh, yh], axis=0),
                    preferred_element_type=F32)
            + jnp.dot(xh, yl, preferred_element_type=F32))


def _dot_sel_rhs(a, sel):
    return sum(jnp.dot(t, sel, preferred_element_type=F32) for t in _split3(a))


def _dot_sel_lhs(sel, b):
    return sum(jnp.dot(sel, t, preferred_element_type=F32) for t in _split3(b))


def _rms(x, gain):
    return x * lax.rsqrt(jnp.mean(x * x, axis=-1, keepdims=True) + EPS) * gain


def _inproj_kernel(x_ref, g_ref, w_ref, o_ref, h_ref):
    @pl.when(pl.program_id(1) == 0)
    def _():
        h_ref[...] = _rms(x_ref[...], g_ref[...]).astype(BF16)

    o_ref[...] = jnp.dot(h_ref[...], w_ref[...], preferred_element_type=F32)


def _in_proj(x2, gain, w_perm, tm=512, tn=2944):
    t = x2.shape[0]
    return pl.pallas_call(
        _inproj_kernel,
        grid=(t // tm, P_WIDTH // tn),
        in_specs=[pl.BlockSpec((tm, D_MODEL), lambda i, j: (i, 0)),
                  pl.BlockSpec((1, D_MODEL), lambda i, j: (0, 0)),
                  pl.BlockSpec((D_MODEL, tn), lambda i, j: (0, j))],
        out_specs=pl.BlockSpec((tm, tn), lambda i, j: (i, j)),
        out_shape=jax.ShapeDtypeStruct((t, P_WIDTH), F32),
        scratch_shapes=[pltpu.VMEM((tm, D_MODEL), BF16)],
        compiler_params=_cparams(2),
        name="in_proj",
    )(x2, gain, w_perm)


def _conv_silu(x, w):
    rows = lax.broadcasted_iota(jnp.int32, x.shape, 0)
    y = x * w[DN_CONV - 1:DN_CONV, :]
    for j in range(DN_CONV - 1):
        sh = DN_CONV - 1 - j
        xs = jnp.where(rows >= sh, pltpu.roll(x, sh, axis=0), 0.0)
        y = y + xs * w[j:j + 1, :]
    return y * jax.nn.sigmoid(y)


def _gdn_kernel(alog_ref, dtb_ref, q_ref, k_ref, v_ref, z_ref, sm_ref, cq_ref, ck_ref, cv_ref,
                onorm_ref, o_ref, qs, ks, vs, gs, bs, us, ws, qks, os_, gls):
    h = pl.program_id(1)
    s_len = q_ref.shape[0]
    c = DN_CHUNK
    n_chunks = s_len // c

    q = _conv_silu(q_ref[...], cq_ref[...])
    k = _conv_silu(k_ref[...], ck_ref[...])
    v = _conv_silu(v_ref[...], cv_ref[...])
    qs[...] = q * lax.rsqrt(jnp.sum(q * q, axis=-1, keepdims=True) + EPS) * (DN_HEAD_DIM ** -0.5)
    ks[...] = k * lax.rsqrt(jnp.sum(k * k, axis=-1, keepdims=True) + EPS)
    vs[...] = v

    kk = lax.broadcasted_iota(jnp.int32, (LANES, 2 * LANES), 0)
    nn = lax.broadcasted_iota(jnp.int32, (LANES, 2 * LANES), 1)
    onehot = (kk == jnp.where(nn < LANES, SMALL_B + h, SMALL_A + h)).astype(BF16)
    ba = _dot_sel_rhs(sm_ref[...], onehot)
    neg_a = -jnp.exp(jnp.full((1, LANES), alog_ref[0, h], F32))
    dtb = jnp.full((1, LANES), dtb_ref[0, h], F32)
    bs[...] = jax.nn.sigmoid(ba[:, :LANES])
    gs[...] = neg_a * jax.nn.softplus(ba[:, LANES:] + dtb)

    ri = lax.broadcasted_iota(jnp.int32, (c, c), 0)
    ci = lax.broadcasted_iota(jnp.int32, (c, c), 1)
    causal = ri >= ci
    strict = ri > ci
    eye = (ri == ci).astype(F32)
    ltri = causal.astype(BF16)

    def intra(ig, carry):
        grp = range(DN_INTRA_GROUP)
        rs = [pl.ds(pl.multiple_of((ig * DN_INTRA_GROUP + j) * c, c), c) for j in grp]
        qc, kc, vc, bc = ([ref[r, :] for r in rs] for ref in (qs, ks, vs, bs))
        gcum = [_dot_sel_lhs(ltri, gs[r, :]) for r in rs]
        decay = [jnp.exp(jnp.where(causal, g - g.T, -jnp.inf)) for g in gcum]
        kb = [kc[j] * bc[j] for j in grp]
        a = [jnp.where(strict, _dot_nt(kb[j], kc[j]) * decay[j], 0.0) for j in grp]
        tinv = [eye - a[j] for j in grp]
        pw = [_hi_lo(x) for x in a]
        for _ in range(int(math.log2(c)) - 1):
            pw = [_hi_lo(_dot_hl(x, x)) for x in pw]
            tinv = [tinv[j] + _dot_hl(_hi_lo(tinv[j]), pw[j]) for j in grp]
        eg = [jnp.exp(g) for g in gcum]
        uw = [_bdot(tinv[j], jnp.concatenate([vc[j] * bc[j], kb[j] * eg[j]], axis=1)) for j in grp]
        qk = [_dot_nt(qc[j], kc[j]) * decay[j] for j in grp]
        for j in grp:
            r = rs[j]
            glast = gcum[j][c - 1:c, :]
            us[r, :] = uw[j][:, :LANES]
            ws[r, :] = uw[j][:, LANES:]
            qks[r, :] = qk[j]
            qs[r, :] = qc[j] * eg[j]
            ks[r, :] = kc[j] * jnp.exp(glast - gcum[j])
            gls[pl.ds(ig * DN_INTRA_GROUP + j, 1), :] = jnp.exp(glast)
        return carry

    lax.fori_loop(0, n_chunks // DN_INTRA_GROUP, intra, 0)

    def inter(ic, state):
        r = pl.ds(pl.multiple_of(ic * c, c), c)
        v_new = us[r, :] - _bdot(ws[r, :], state)
        os_[r, :] = _bdot(qs[r, :], state) + _bdot(qks[r, :], v_new)
        return state * gls[pl.ds(ic, 1), :] + _dot_tn(ks[r, :], v_new)

    lax.fori_loop(0, n_chunks, inter, jnp.zeros((DN_HEAD_DIM, DN_HEAD_DIM), F32))

    z = z_ref[...]
    o_ref[...] = _rms(os_[...], onorm_ref[...]) * (z * jax.nn.sigmoid(z))


def _gdn(p, conv_w, a_log, dt_bias, out_norm, batch, seq):
    row_blk = lambda col0: pl.BlockSpec((seq, LANES), lambda b, h, col0=col0: (b, col0 + h))
    conv_blk = lambda col0: pl.BlockSpec((DN_CONV, LANES), lambda b, h, col0=col0: (0, col0 + h))
    smem = pl.BlockSpec(memory_space=pltpu.SMEM)
    return pl.pallas_call(
        _gdn_kernel,
        grid=(batch, DN_HEADS),
        in_specs=[smem, smem,
                  row_blk(COL_Q), row_blk(COL_K), row_blk(COL_V), row_blk(COL_Z),
                  pl.BlockSpec((seq, LANES), lambda b, h: (b, COL_SMALL)),
                  conv_blk(COL_Q), conv_blk(COL_K), conv_blk(COL_V),
                  pl.BlockSpec((1, LANES), lambda b, h: (0, 0))],
        out_specs=pl.BlockSpec((seq, LANES), lambda b, h: (b, h)),
        out_shape=jax.ShapeDtypeStruct((batch * seq, DN_W), F32),
        scratch_shapes=([pltpu.VMEM((seq, LANES), F32) for _ in range(9)]
                        + [pltpu.VMEM((seq // DN_CHUNK, LANES), F32)]),
        compiler_params=_cparams(2),
        name="gdn",
    )(a_log, dt_bias, p, p, p, p, p, conv_w, conv_w, conv_w, out_norm)


def _gelu_tanh(x):
    return 0.5 * x * (1.0 + jnp.tanh(math.sqrt(2.0 / math.pi) * (x + 0.044715 * (x * x * x))))


def _compress_one(t2, pos, w1, w2):
    n = t2.shape[0]
    half = t2.shape[1]
    y1 = _bdot(t2 + pos[:, :half], w1[:half, :])
    y2 = _bdot(t2 + pos[:, half:], w1[half:, :])
    hid = _gelu_tanh(y1 + pltpu.roll(y2, n - 1, axis=0))
    return _bdot(hid, w2)


def _compress_kernel(tk_ref, tv_ref, pk_ref, pv_ref, w1k_ref, w2k_ref, w1v_ref, w2v_ref, kn_ref,
                     kc_ref, vc_ref):
    kc = _compress_one(tk_ref[0, 0], pk_ref[...], w1k_ref[...], w2k_ref[...])
    kc_ref[0, 0] = _rms(kc, kn_ref[...])
    vc_ref[0, 0] = _compress_one(tv_ref[0, 0], pv_ref[...], w1v_ref[...], w2v_ref[...])


def _compress(t2k, t2v, pos_k, pos_v, w1k, w2k, w1v, w2v, kn_cmp):
    batch, groups, n, width = t2k.shape
    d = NSA_HEAD_DIM
    tok = pl.BlockSpec((1, 1, n, width), lambda b, g: (b, g, 0, 0))
    full = lambda a: pl.BlockSpec(a.shape, lambda b, g: (0,) * a.ndim)
    out = pl.BlockSpec((1, 1, n, d), lambda b, g: (b, g, 0, 0))
    return pl.pallas_call(
        _compress_kernel,
        grid=(batch, groups),
        in_specs=[tok, tok, full(pos_k), full(pos_v), full(w1k), full(w2k), full(w1v), full(w2v),
                  full(kn_cmp)],
        out_specs=[out, out],
        out_shape=[jax.ShapeDtypeStruct((batch, groups, n, d), F32)] * 2,
        compiler_params=_cparams(2),
        name="nsa_compress",
    )(t2k, t2v, pos_k, pos_v, w1k, w2k, w1v, w2v, kn_cmp)


def _nsa_kernel(q_ref, ks_ref, vs_ref, kw_ref, vw_ref, kc_ref, vc_ref, sm_ref, qn_ref, kns_ref,
                knw_ref, ovt_ref, o_ref, ksa, vsa, kwa, vwa, kca, vca, m_slc, acc_slc, m_win, acc_win):
    it = pl.program_id(1)
    tile = ATT_TILE
    s_len = ks_ref.shape[0]
    d = NSA_HEAD_DIM
    rep = NSA_REP
    rows = rep * tile
    n_sel = s_len // SLC_LEN
    n_cend = kc_ref.shape[2]

    @pl.when(it == 0)
    def _():
        rowi = lax.broadcasted_iota(jnp.int32, (s_len, d), 0)
        xl = lax.broadcasted_iota(jnp.int32, (s_len, d), 1)
        in_tile = (rowi & (tile - 1)).astype(F32)
        pos_cols = jnp.where((xl == X_SLOPE) | (xl == X_SLOPE + 1), in_tile, 0.0)
        blk_cols = jnp.where(xl == lax.shift_right_logical(rowi, int(math.log2(SLC_LEN))), 1.0, 0.0)
        ci = lax.broadcasted_iota(jnp.int32, (n_cend, d), 0)
        cl = lax.broadcasted_iota(jnp.int32, (n_cend, d), 1)
        cend_cols = jnp.where((cl == X_SLOPE) | (cl == X_SLOPE + 1), (CMP_STRIDE * ci).astype(F32), 0.0)
        one_col = jnp.where(xl == 0, 1.0, 0.0)
        for g in range(NSA_GROUPS):
            sl = slice(g * d, (g + 1) * d)
            ksa[g] = jnp.concatenate([_rms(ks_ref[:, sl], kns_ref[...]), pos_cols + blk_cols],
                                     axis=1).astype(BF16)
            kwa[g] = jnp.concatenate([_rms(kw_ref[:, sl], knw_ref[...]), pos_cols], axis=1).astype(BF16)
            kca[g] = jnp.concatenate([kc_ref[0, g], cend_cols], axis=1).astype(BF16)
            vsa[g] = jnp.concatenate([vs_ref[:, sl], one_col], axis=1).astype(BF16)
            vwa[g] = jnp.concatenate([vw_ref[:, sl], one_col], axis=1).astype(BF16)
            vca[g] = jnp.concatenate([vc_ref[0, g], jnp.zeros((n_cend, d), F32)], axis=1).astype(BF16)

    t0 = it * tile
    groups = range(NSA_GROUPS)
    rowi = lax.broadcasted_iota(jnp.int32, (rows, 1), 0)
    row_t = rowi & (tile - 1)
    row_r = lax.shift_right_logical(lax.broadcasted_iota(jnp.int32, (rows, LANES), 0),
                                    int(math.log2(tile)))
    tpos = t0 + row_t
    xl = lax.broadcasted_iota(jnp.int32, (rows, d), 1)
    col = lax.broadcasted_iota(jnp.int32, (rows, tile), 1)
    lower = col <= row_t
    upper = col > row_t
    cend = CMP_STRIDE * lax.broadcasted_iota(jnp.int32, (1, LANES), 1) + (CMP_LEN - 1)
    cmp_ok = tpos >= cend
    any_valid = (tpos >= CMP_LEN - 1).astype(F32)
    gates = jax.nn.sigmoid(sm_ref[...])

    def attend(items):
        s = [_dot_nt(x[0], x[1]) for x in items]
        s = [si if x[4] is None else jnp.where(x[4], si, NEG) for si, x in zip(s, items)]
        m_old = [x[5][x[7]] for x in items]
        m_new = [jnp.maximum(mo, jnp.broadcast_to(jnp.max(si, axis=-1, keepdims=True), mo.shape) + x[3])
                 for si, mo, x in zip(s, m_old, items)]
        alpha = [jnp.exp(mo - mn) for mo, mn in zip(m_old, m_new)]
        p = [jnp.exp(si - jnp.concatenate([mn - x[3]] * (tile // LANES), axis=1))
             for si, mn, x in zip(s, m_new, items)]
        pv = [_bdot(pi, x[2]) for pi, x in zip(p, items)]
        for x, al, mn, pvi in zip(items, alpha, m_new, pv):
            x[6][x[7]] = al * x[6][x[7]] + pvi
            x[5][x[7]] = mn

    def normalised(acc_ref, g):
        acc = acc_ref[g]
        return acc[:, :d] / acc[:, d:d + 1]

    slope, q64, q_x, q_aug = [], [], [], []
    for g in groups:
        slopes = [2.0 ** (-8.0 * (g * rep + r + 1) / NSA_HEADS) for r in range(rep)]
        sl = jnp.full((rows, LANES), slopes[rep - 1], F32)
        for r in range(rep - 2, -1, -1):
            sl = jnp.where(row_r == r, slopes[r], sl)
        sl_hi = sl.astype(BF16).astype(F32)[:, :d]
        slope.append(sl)
        q_x.append(jnp.where(xl == X_SLOPE, sl_hi, jnp.where(xl == X_SLOPE + 1, sl[:, :d] - sl_hi, 0.0)))
        q64.append(jnp.concatenate(
            [_rms(q_ref[:, (g * rep + r) * d:(g * rep + r + 1) * d], qn_ref[...]) * (d ** -0.5)
             for r in range(rep)], axis=0))
        q_aug.append(jnp.concatenate([q64[g], q_x[g]], axis=1).astype(BF16))

    s_c = [jnp.where(cmp_ok, _dot_nt(q_aug[g], kca[g]), NEG) for g in groups]
    e_c = [jnp.exp(s - jnp.max(s, axis=-1, keepdims=True)) for s in s_c]
    p_c = [e / jnp.sum(e, axis=-1, keepdims=True) * any_valid for e in e_c]
    o_cmp = [_bdot(p_c[g], vca[g])[:, :d] for g in groups]
    p_sum = [sum(p[r * tile:(r + 1) * tile] for r in range(rep)) for p in p_c]

    jj = lax.broadcasted_iota(jnp.int32, (X_SLOPE, tile), 0)
    blk_t = lax.shift_right_logical(t0 + lax.broadcasted_iota(jnp.int32, (1, tile), 1), int(math.log2(SLC_LEN)))
    imp = [sum(_dot_nt(ovt_ref[...], t) for t in _split3(ps))[:X_SLOPE] for ps in p_sum]
    imp = [jnp.where(jj <= blk_t, x, NEG) for x in imp]
    imp = [jnp.where((jj == blk_t) | (jj == 0), FORCE, x) for x in imp]
    imp = [jnp.where(jj < n_sel, x, -3e38) for x in imp]
    rank = [jnp.zeros((X_SLOPE, tile), F32) for _ in groups]
    for j in range(n_sel):
        for g in groups:
            row = imp[g][j:j + 1, :]
            ge = jnp.where(row >= imp[g], 1.0, 0.0)
            gt = jnp.where(row > imp[g], 1.0, 0.0)
            rank[g] = rank[g] + jnp.where(jj > j, ge, gt)
    k_sel = float(min(SLC_TOPK, n_sel))
    bias_t = [jnp.where((rk < k_sel) | (jj >= n_sel), 0.0, NEG) for rk in rank]
    zero_rows = jnp.zeros((LANES - X_SLOPE, tile), F32)
    sel_bias = [jnp.concatenate([bt, zero_rows], axis=0).T[:, :d] for bt in bias_t]
    q_sel = [jnp.concatenate([q64[g], q_x[g] + jnp.concatenate([sel_bias[g]] * rep, axis=0)],
                             axis=1).astype(BF16) for g in groups]

    for g in groups:
        m_slc[g] = jnp.full((rows, LANES), -jnp.inf, F32)
        m_win[g] = jnp.full((rows, LANES), -jnp.inf, F32)
        acc_slc[g] = jnp.zeros((rows, LANES), F32)
        acc_win[g] = jnp.zeros((rows, LANES), F32)

    r_diag = pl.ds(pl.multiple_of(t0, tile), tile)
    t0f = t0.astype(F32)
    attend([(q_sel[g], ksa[g, r_diag, :], vsa[g, r_diag, :], slope[g] * t0f, lower, m_slc, acc_slc, g)
            for g in groups])
    attend([(q_aug[g], kwa[g, r_diag, :], vwa[g, r_diag, :], slope[g] * t0f, lower, m_win, acc_win, g)
            for g in groups])

    @pl.when(it > 0)
    def _():
        r_prev = pl.ds(pl.multiple_of(t0 - tile, tile), tile)
        attend([(q_aug[g], kwa[g, r_prev, :], vwa[g, r_prev, :], slope[g] * (t0f - tile), upper,
                 m_win, acc_win, g) for g in groups])

    def slc_tile(kt, carry):
        r = pl.ds(pl.multiple_of(kt * tile, tile), tile)
        ktf = (kt * tile).astype(F32)
        attend([(q_sel[g], ksa[g, r, :], vsa[g, r, :], slope[g] * ktf, None, m_slc, acc_slc, g)
                for g in groups])
        return carry

    lax.fori_loop(0, it, slc_tile, 0)

    for g in groups:
        o_slc = normalised(acc_slc, g)
        o_win = normalised(acc_win, g)
        for r in range(rep):
            hd = g * rep + r
            rs = slice(r * tile, (r + 1) * tile)
            c0 = SMALL_GATE + 3 * hd
            o = (gates[:, c0:c0 + 1] * o_cmp[g][rs] + gates[:, c0 + 1:c0 + 2] * o_slc[rs]
                 + gates[:, c0 + 2:c0 + 3] * o_win[rs])
            o_ref[:, hd * d:(hd + 1) * d] = o


def _nsa(p, kc, vc, q_norm, kn_slc, kn_win, ovt, batch, seq):
    tile = ATT_TILE
    nt = seq // tile
    rows = NSA_REP * tile
    kv = lambda col: pl.BlockSpec((seq, NSA_KV_W), lambda b, i, col=col: (b, col // 2),
                                  pipeline_mode=pl.Buffered(1))
    full = lambda a: pl.BlockSpec(a.shape, lambda b, i: (0,) * a.ndim)
    cmp_blk = pl.BlockSpec((1,) + kc.shape[1:], lambda b, i: (b, 0, 0, 0))
    aug = lambda n: pltpu.VMEM((NSA_GROUPS, n, 2 * NSA_HEAD_DIM), BF16)
    stat = pltpu.VMEM((NSA_GROUPS, rows, LANES), F32)
    return pl.pallas_call(
        _nsa_kernel,
        grid=(batch, nt),
        in_specs=[pl.BlockSpec((tile, NSA_W), lambda b, i: (b * nt + i, COL_NSQ // 8)),
                  kv(COL_KS), kv(COL_VS), kv(COL_KW), kv(COL_VW),
                  cmp_blk, cmp_blk,
                  pl.BlockSpec((tile, LANES), lambda b, i: (b * nt + i, COL_SMALL)),
                  full(q_norm), full(kn_slc), full(kn_win), full(ovt)],
        out_specs=pl.BlockSpec((tile, NSA_W), lambda b, i: (b * nt + i, 0)),
        out_shape=jax.ShapeDtypeStruct((batch * seq, NSA_W), F32),
        scratch_shapes=[aug(seq), aug(seq), aug(seq), aug(seq), aug(kc.shape[2]), aug(kc.shape[2]),
                        stat, stat, stat, stat],
        compiler_params=_cparams(2),
        name="nsa_attention",
    )(p, p, p, p, p, kc, vc, p, q_norm, kn_slc, kn_win, ovt)


def _mix_kernel(x_ref, odn_ref, ons_ref, gdn_ref, gns_ref, wdn_ref, wns_ref, wout_ref, o_ref):
    y_dn = jnp.dot(odn_ref[...].astype(BF16), wdn_ref[...], preferred_element_type=F32)
    y_ns = jnp.dot(ons_ref[...].astype(BF16), wns_ref[...], preferred_element_type=F32)
    mix = jax.nn.sigmoid(gdn_ref[...]) * y_dn + jax.nn.sigmoid(gns_ref[...]) * y_ns
    o_ref[...] = x_ref[...] + jnp.dot(mix.astype(BF16), wout_ref[...], preferred_element_type=F32)


def _mix(x2, o_dn, o_ns, p, w_dn, w_ns, w_out, tm=512):
    t = x2.shape[0]
    row = lambda col: pl.BlockSpec((tm, D_MODEL), lambda i, col=col: (i, col))
    wfull = pl.BlockSpec((D_MODEL, D_MODEL), lambda i: (0, 0))
    return pl.pallas_call(
        _mix_kernel,
        grid=(t // tm,),
        in_specs=[row(0), row(0), row(0), row(COL_GDN // 8), row(COL_GNS // 8), wfull, wfull, wfull],
        out_specs=row(0),
        out_shape=jax.ShapeDtypeStruct((t, D_MODEL), F32),
        compiler_params=_cparams(1),
        name="mix_out",
    )(x2, o_dn, o_ns, p, p, w_dn, w_ns, w_out)


def _ffn_kernel(x_ref, g_ref, wup_ref, wdown_ref, o_ref, h_ref, acc_ref):
    f = pl.program_id(1)

    @pl.when(f == 0)
    def _():
        h_ref[...] = _rms(x_ref[...], g_ref[...]).astype(BF16)
        acc_ref[...] = jnp.zeros_like(acc_ref)

    u = jnp.dot(h_ref[...], wup_ref[...], preferred_element_type=F32)
    u = jnp.square(jnp.maximum(u, 0.0)).astype(BF16)
    acc_ref[...] += jnp.dot(u, wdown_ref[...], preferred_element_type=F32)

    @pl.when(f == pl.num_programs(1) - 1)
    def _():
        o_ref[...] = x_ref[...] + acc_ref[...]


def _ffn(x2, gain, w_up, w_down, tm=512, tf=1024):
    t = x2.shape[0]
    return pl.pallas_call(
        _ffn_kernel,
        grid=(t // tm, D_FF // tf),
        in_specs=[pl.BlockSpec((tm, D_MODEL), lambda i, f: (i, 0)),
                  pl.BlockSpec((1, D_MODEL), lambda i, f: (0, 0)),
                  pl.BlockSpec((D_MODEL, tf), lambda i, f: (0, f)),
                  pl.BlockSpec((tf, D_MODEL), lambda i, f: (f, 0))],
        out_specs=pl.BlockSpec((tm, D_MODEL), lambda i, f: (i, 0)),
        out_shape=jax.ShapeDtypeStruct((t, D_MODEL), F32),
        scratch_shapes=[pltpu.VMEM((tm, D_MODEL), BF16), pltpu.VMEM((tm, D_MODEL), F32)],
        compiler_params=_cparams(2),
        name="ffn",
    )(x2, gain, w_up, w_down)


def _overlap_matrix_t(seq):
    n_cmp = (seq - CMP_LEN) // CMP_STRIDE + 1
    n_sel = seq // SLC_LEN
    c0 = np.arange(n_cmp)[None, :] * CMP_STRIDE
    j0 = np.arange(n_sel)[:, None] * SLC_LEN
    ov = np.clip(np.minimum(c0 + CMP_LEN, j0 + SLC_LEN) - np.maximum(c0, j0), 0, None) / CMP_LEN
    out = np.zeros((LANES, LANES), np.float32)
    out[:n_sel, :n_cmp] = ov
    return jnp.asarray(out, BF16)


def _regroup_w_in(w):
    o_b = 4 * DN_W
    o_q = o_b + 2 * DN_HEADS
    o_kv = o_q + NSA_W
    o_gate = o_kv + 6 * NSA_KV_W
    o_gdn = o_gate + 3 * NSA_HEADS
    pad = jnp.zeros((w.shape[0], LANES - 2 * DN_HEADS - 3 * NSA_HEADS), w.dtype)
    return jnp.concatenate(
        [w[:, :o_b], w[:, o_q:o_kv], w[:, o_gdn:], w[:, o_kv:o_gate], w[:, o_b:o_q],
         w[:, o_gate:o_gdn], pad], axis=1).astype(BF16)


def _blocks16(p, col, batch, seq):
    t = p[:, col * LANES:col * LANES + NSA_KV_W]
    t = t.reshape(batch, seq // CMP_STRIDE, CMP_STRIDE, NSA_GROUPS, NSA_HEAD_DIM)
    t = jnp.transpose(t, (0, 3, 1, 2, 4))
    return t.reshape(batch, NSA_GROUPS, seq // CMP_STRIDE, CMP_STRIDE * NSA_HEAD_DIM)


def kernel(x, norm_mix, w_in, dn_conv, dn_a_log, dn_dt_bias, dn_out_norm, nsa_q_norm, nsa_k_norm_cmp,
           nsa_k_norm_slc, nsa_k_norm_win, cmp_pos_k, cmp_w1_k, cmp_w2_k, cmp_pos_v, cmp_w1_v, cmp_w2_v,
           w_proj_dn, w_proj_nsa, w_out, norm_mlp, w_up, w_down):
    batch, seq, _ = x.shape
    assert seq // CMP_STRIDE == LANES and seq % ATT_TILE == 0 and seq // SLC_LEN <= X_SLOPE
    ovt = _overlap_matrix_t(seq)
    x2 = x.reshape(batch * seq, D_MODEL)
    for l in range(w_in.shape[0]):
        p = _in_proj(x2, norm_mix[l][None], _regroup_w_in(w_in[l]))
        o_dn = _gdn(p, dn_conv[l], dn_a_log[l][None], dn_dt_bias[l][None], dn_out_norm[l][None],
                    batch, seq)
        flat = lambda a: a.reshape(1, -1)
        kc, vc = _compress(_blocks16(p, COL_KC, batch, seq), _blocks16(p, COL_VC, batch, seq),
                           flat(cmp_pos_k[l]), flat(cmp_pos_v[l]),
                           cmp_w1_k[l].astype(BF16), cmp_w2_k[l].astype(BF16),
                           cmp_w1_v[l].astype(BF16), cmp_w2_v[l].astype(BF16),
                           nsa_k_norm_cmp[l][None])
        o_ns = _nsa(p, kc, vc, nsa_q_norm[l][None], nsa_k_norm_slc[l][None], nsa_k_norm_win[l][None],
                    ovt, batch, seq)
        x2 = _mix(x2, o_dn, o_ns, p, w_proj_dn[l].astype(BF16), w_proj_nsa[l].astype(BF16),
                  w_out[l].astype(BF16))
        x2 = _ffn(x2, norm_mlp[l][None], w_up[l].astype(BF16), w_down[l].astype(BF16))
    return x2.reshape(batch, seq, D_MODEL)
```

```python
import functools
import math

import numpy as np
import jax
import jax.numpy as jnp
from jax import lax
from jax.experimental import pallas as pl
from jax.experimental.pallas import tpu as pltpu

F32 = jnp.float32
BF16 = jnp.bfloat16

D_MODEL = 1024
DN_HEADS = 8
DN_HEAD_DIM = 128
DN_W = DN_HEADS * DN_HEAD_DIM
DN_CONV = 4
NSA_HEADS = 16
NSA_GROUPS = 4
NSA_REP = NSA_HEADS // NSA_GROUPS
NSA_HEAD_DIM = 64
NSA_W = NSA_HEADS * NSA_HEAD_DIM
NSA_KV_W = NSA_GROUPS * NSA_HEAD_DIM
CMP_LEN = 32
CMP_STRIDE = 16
CMP_HIDDEN = 2 * NSA_HEAD_DIM
SLC_LEN = 64
SLC_TOPK = 8
WINDOW = 256
D_FF = 4 * D_MODEL
EPS = 1e-6
NEG = -1e30
FORCE = 1e9
LANES = 128

DN_CHUNK = LANES
DN_INTRA_GROUP = 4
ATT_TILE = WINDOW
ATT_ROW_BLOCK = 128

COL_Q, COL_K, COL_V, COL_Z = 0, 8, 16, 24
COL_NSQ, COL_GDN, COL_GNS = 32, 40, 48
COL_KC, COL_VC, COL_KS, COL_VS, COL_KW, COL_VW = 56, 58, 60, 62, 64, 66
COL_SMALL = 68
P_WIDTH = 72 * LANES
SMALL_B, SMALL_A, SMALL_GATE = 0, DN_HEADS, 2 * DN_HEADS

X_SLOPE = SLC_TOPK * 4

VMEM_LIMIT = 56 * 1024 * 1024


def _cparams(n_axes):
    return pltpu.CompilerParams(dimension_semantics=("arbitrary",) * n_axes,
                                vmem_limit_bytes=VMEM_LIMIT)


def _bdot(a, b):
    return jnp.dot(a.astype(BF16), b.astype(BF16), preferred_element_type=F32)


def _dot_nt(a, b):
    return lax.dot_general(a.astype(BF16), b.astype(BF16), (((1,), (1,)), ((), ())),
                           preferred_element_type=F32)


def _dot_tn(a, b):
    return lax.dot_general(a.astype(BF16), b.astype(BF16), (((0,), (0,)), ((), ())),
                           preferred_element_type=F32)


def _split3(a):
    a1 = a.astype(BF16)
    r = a - a1.astype(F32)
    a2 = r.astype(BF16)
    a3 = (r - a2.astype(F32)).astype(BF16)
    return a1, a2, a3


def _hi_lo(a):
    hi = a.astype(BF16)
    return hi, (a - hi.astype(F32)).astype(BF16)


def _dot_hl(x, y):
    (xh, xl), (yh, yl) = x, y
    return (jnp.dot(jnp.concatenate([xh, xl], axis=1), jnp.concatenate([yh, yh], axis=0),
                    preferred_element_type=F32)
            + jnp.dot(xh, yl, preferred_element_type=F32))


def _dot_sel_rhs(a, sel):
    return sum(jnp.dot(t, sel, preferred_element_type=F32) for t in _split3(a))


def _dot_sel_lhs(sel, b):
    return sum(jnp.dot(sel, t, preferred_element_type=F32) for t in _split3(b))


def _rms(x, gain):
    return x * lax.rsqrt(jnp.mean(x * x, axis=-1, keepdims=True) + EPS) * gain


def _inproj_kernel(x_ref, g_ref, w_ref, o_ref, h_ref):
    @pl.when(pl.program_id(1) == 0)
    def _():
        h_ref[...] = _rms(x_ref[...], g_ref[...]).astype(BF16)

    o_ref[...] = jnp.dot(h_ref[...], w_ref[...], preferred_element_type=F32)


def _in_proj(x2, gain, w_perm, tm=1024, tn=1536):
    t = x2.shape[0]
    return pl.pallas_call(
        _inproj_kernel,
        grid=(t // tm, P_WIDTH // tn),
        in_specs=[pl.BlockSpec((tm, D_MODEL), lambda i, j: (i, 0)),
                  pl.BlockSpec((1, D_MODEL), lambda i, j: (0, 0)),
                  pl.BlockSpec((D_MODEL, tn), lambda i, j: (0, j))],
        out_specs=pl.BlockSpec((tm, tn), lambda i, j: (i, j)),
        out_shape=jax.ShapeDtypeStruct((t, P_WIDTH), F32),
        scratch_shapes=[pltpu.VMEM((tm, D_MODEL), BF16)],
        compiler_params=_cparams(2),
        name="in_proj",
    )(x2, gain, w_perm)


def _conv_silu(x, w):
    rows = lax.broadcasted_iota(jnp.int32, x.shape, 0)
    y = x * w[DN_CONV - 1:DN_CONV, :]
    for j in range(DN_CONV - 1):
        sh = DN_CONV - 1 - j
        xs = jnp.where(rows >= sh, pltpu.roll(x, sh, axis=0), 0.0)
        y = y + xs * w[j:j + 1, :]
    return y * jax.nn.sigmoid(y)


def _gdn_kernel(alog_ref, dtb_ref, q_ref, k_ref, v_ref, z_ref, sm_ref, cq_ref, ck_ref, cv_ref,
                onorm_ref, o_ref, qs, ks, vs, gs, bs, us, ws, qks, os_, gls):
    h = pl.program_id(1)
    s_len = q_ref.shape[0]
    c = DN_CHUNK
    n_chunks = s_len // c

    q = _conv_silu(q_ref[...], cq_ref[...])
    k = _conv_silu(k_ref[...], ck_ref[...])
    v = _conv_silu(v_ref[...], cv_ref[...])
    qs[...] = q * lax.rsqrt(jnp.sum(q * q, axis=-1, keepdims=True) + EPS) * (DN_HEAD_DIM ** -0.5)
    ks[...] = k * lax.rsqrt(jnp.sum(k * k, axis=-1, keepdims=True) + EPS)
    vs[...] = v

    kk = lax.broadcasted_iota(jnp.int32, (LANES, 2 * LANES), 0)
    nn = lax.broadcasted_iota(jnp.int32, (LANES, 2 * LANES), 1)
    onehot = (kk == jnp.where(nn < LANES, SMALL_B + h, SMALL_A + h)).astype(BF16)
    ba = _dot_sel_rhs(sm_ref[...], onehot)
    neg_a = -jnp.exp(jnp.full((1, LANES), alog_ref[0, h], F32))
    dtb = jnp.full((1, LANES), dtb_ref[0, h], F32)
    bs[...] = jax.nn.sigmoid(ba[:, :LANES])
    gs[...] = neg_a * jax.nn.softplus(ba[:, LANES:] + dtb)

    ri = lax.broadcasted_iota(jnp.int32, (c, c), 0)
    ci = lax.broadcasted_iota(jnp.int32, (c, c), 1)
    causal = ri >= ci
    strict = ri > ci
    eye = (ri == ci).astype(F32)
    ltri = causal.astype(BF16)

    def intra(ig, carry):
        grp = range(DN_INTRA_GROUP)
        rs = [pl.ds(pl.multiple_of((ig * DN_INTRA_GROUP + j) * c, c), c) for j in grp]
        qc, kc, vc, bc = ([ref[r, :] for r in rs] for ref in (qs, ks, vs, bs))
        gcum = [_dot_sel_lhs(ltri, gs[r, :]) for r in rs]
        decay = [jnp.exp(jnp.where(causal, g - g.T, -jnp.inf)) for g in gcum]
        kb = [kc[j] * bc[j] for j in grp]
        a = [jnp.where(strict, _dot_nt(kb[j], kc[j]) * decay[j], 0.0) for j in grp]
        tinv = [eye - a[j] for j in grp]
        pw = [_hi_lo(x) for x in a]
        for _ in range(int(math.log2(c)) - 1):
            pw = [_hi_lo(_dot_hl(x, x)) for x in pw]
            tinv = [tinv[j] + _dot_hl(_hi_lo(tinv[j]), pw[j]) for j in grp]
        eg = [jnp.exp(g) for g in gcum]
        uw = [_bdot(tinv[j], jnp.concatenate([vc[j] * bc[j], kb[j] * eg[j]], axis=1)) for j in grp]
        qk = [_dot_nt(qc[j], kc[j]) * decay[j] for j in grp]
        glast = [g[c - 1:c, :] for g in gcum]
        k_end = [kc[j] * jnp.exp(glast[j] - gcum[j]) for j in grp]
        qk_uw = [_bdot(qk[j], uw[j]) for j in grp]
        ke_uw = [_dot_tn(k_end[j], uw[j]) for j in grp]
        for j in grp:
            r = rs[j]
            qs[r, :] = qc[j] * eg[j] - qk_uw[j][:, LANES:]
            qks[r, :] = qk_uw[j][:, :LANES]
            ws[r, :] = ke_uw[j][:, LANES:]
            us[r, :] = ke_uw[j][:, :LANES]
            gls[pl.ds(ig * DN_INTRA_GROUP + j, 1), :] = jnp.exp(glast[j])
        return carry

    lax.fori_loop(0, n_chunks // DN_INTRA_GROUP, intra, 0)

    def inter(ic, state):
        r = pl.ds(pl.multiple_of(ic * c, c), c)
        os_[r, :] = _bdot(qs[r, :], state) + qks[r, :]
        return state * gls[pl.ds(ic, 1), :] - _bdot(ws[r, :], state) + us[r, :]

    lax.fori_loop(0, n_chunks, inter, jnp.zeros((DN_HEAD_DIM, DN_HEAD_DIM), F32))

    z = z_ref[...]
    o_ref[...] = _rms(os_[...], onorm_ref[...]) * (z * jax.nn.sigmoid(z))


def _gdn(p, conv_w, a_log, dt_bias, out_norm, batch, seq):
    row_blk = lambda col0: pl.BlockSpec((seq, LANES), lambda b, h, col0=col0: (b, col0 + h))
    conv_blk = lambda col0: pl.BlockSpec((DN_CONV, LANES), lambda b, h, col0=col0: (0, col0 + h))
    smem = pl.BlockSpec(memory_space=pltpu.SMEM)
    return pl.pallas_call(
        _gdn_kernel,
        grid=(batch, DN_HEADS),
        in_specs=[smem, smem,
                  row_blk(COL_Q), row_blk(COL_K), row_blk(COL_V), row_blk(COL_Z),
                  pl.BlockSpec((seq, LANES), lambda b, h: (b, COL_SMALL)),
                  conv_blk(COL_Q), conv_blk(COL_K), conv_blk(COL_V),
                  pl.BlockSpec((1, LANES), lambda b, h: (0, 0))],
        out_specs=pl.BlockSpec((seq, LANES), lambda b, h: (b, h)),
        out_shape=jax.ShapeDtypeStruct((batch * seq, DN_W), F32),
        scratch_shapes=([pltpu.VMEM((seq, LANES), F32) for _ in range(9)]
                        + [pltpu.VMEM((seq // DN_CHUNK, LANES), F32)]),
        compiler_params=_cparams(2),
        name="gdn",
    )(a_log, dt_bias, p, p, p, p, p, conv_w, conv_w, conv_w, out_norm)


def _gelu_tanh(x):
    return 0.5 * x * (1.0 + jnp.tanh(math.sqrt(2.0 / math.pi) * (x + 0.044715 * (x * x * x))))


def _compress_one(t2, pos, w1, w2):
    n = t2.shape[0]
    half = t2.shape[1]
    y1 = _bdot(t2 + pos[:, :half], w1[:half, :])
    y2 = _bdot(t2 + pos[:, half:], w1[half:, :])
    hid = _gelu_tanh(y1 + pltpu.roll(y2, n - 1, axis=0))
    return _bdot(hid, w2)


def _compress_kernel(tk_ref, tv_ref, pk_ref, pv_ref, w1k_ref, w2k_ref, w1v_ref, w2v_ref, kn_ref,
                     kc_ref, vc_ref):
    kc = _compress_one(tk_ref[0, 0], pk_ref[...], w1k_ref[...], w2k_ref[...])
    kc_ref[0, 0] = _rms(kc, kn_ref[...])
    vc_ref[0, 0] = _compress_one(tv_ref[0, 0], pv_ref[...], w1v_ref[...], w2v_ref[...])


def _compress(t2k, t2v, pos_k, pos_v, w1k, w2k, w1v, w2v, kn_cmp):
    batch, groups, n, width = t2k.shape
    d = NSA_HEAD_DIM
    tok = pl.BlockSpec((1, 1, n, width), lambda b, g: (b, g, 0, 0))
    full = lambda a: pl.BlockSpec(a.shape, lambda b, g: (0,) * a.ndim)
    out = pl.BlockSpec((1, 1, n, d), lambda b, g: (b, g, 0, 0))
    return pl.pallas_call(
        _compress_kernel,
        grid=(batch, groups),
        in_specs=[tok, tok, full(pos_k), full(pos_v), full(w1k), full(w2k), full(w1v), full(w2v),
                  full(kn_cmp)],
        out_specs=[out, out],
        out_shape=[jax.ShapeDtypeStruct((batch, groups, n, d), F32)] * 2,
        compiler_params=_cparams(2),
        name="nsa_compress",
    )(t2k, t2v, pos_k, pos_v, w1k, w2k, w1v, w2v, kn_cmp)


def _nsa_kernel(q_ref, ks_ref, vs_ref, kw_ref, vw_ref, kc_ref, vc_ref, sm_ref, qn_ref, kns_ref,
                knw_ref, ovt_ref, gsel_ref, hsum_ref, o_ref, ksa, vsa, kwa, vwa, kca, vca, m_slc, acc_slc,
                m_win, acc_win):
    it = pl.program_id(1)
    tile = ATT_TILE
    s_len = ks_ref.shape[0]
    d = NSA_HEAD_DIM
    rep = NSA_REP
    rows = rep * tile
    n_sel = s_len // SLC_LEN
    n_cend = kc_ref.shape[2]

    @pl.when(it == 0)
    def _():
        rowi = lax.broadcasted_iota(jnp.int32, (s_len, d), 0)
        xl = lax.broadcasted_iota(jnp.int32, (s_len, d), 1)
        in_tile = (rowi & (tile - 1)).astype(F32)
        tile_start = (rowi - (rowi & (tile - 1))).astype(F32)
        pos_cols = jnp.where((xl == X_SLOPE) | (xl == X_SLOPE + 1), in_tile,
                             jnp.where((xl == X_SLOPE + 2) | (xl == X_SLOPE + 3), tile_start, 0.0))
        blk_cols = jnp.where(xl == lax.shift_right_logical(rowi, int(math.log2(SLC_LEN))), 1.0, 0.0)
        ci = lax.broadcasted_iota(jnp.int32, (n_cend, d), 0)
        cl = lax.broadcasted_iota(jnp.int32, (n_cend, d), 1)
        cend_cols = jnp.where((cl == X_SLOPE) | (cl == X_SLOPE + 1), (CMP_STRIDE * ci).astype(F32), 0.0)
        ones_s = jnp.ones((s_len, LANES), F32)
        ones_c = jnp.ones((n_cend, LANES), F32)
        for g in range(NSA_GROUPS):
            sl = slice(g * d, (g + 1) * d)
            ksa[g] = jnp.concatenate([_rms(ks_ref[:, sl], kns_ref[...]), pos_cols + blk_cols],
                                     axis=1).astype(BF16)
            kwa[g] = jnp.concatenate([_rms(kw_ref[:, sl], knw_ref[...]), pos_cols], axis=1).astype(BF16)
            kca[g] = jnp.concatenate([kc_ref[0, g], cend_cols], axis=1).astype(BF16)
            vsa[g] = jnp.concatenate([vs_ref[:, sl], vs_ref[:, sl], ones_s], axis=1).astype(BF16)
            vwa[g] = jnp.concatenate([vw_ref[:, sl], vw_ref[:, sl], ones_s], axis=1).astype(BF16)
            vca[g] = jnp.concatenate([vc_ref[0, g], vc_ref[0, g], ones_c], axis=1).astype(BF16)

    t0 = it * tile
    groups = range(NSA_GROUPS)
    row_r =lax.shift_right_logical(lax.broadcasted_iota(jnp.int32, (rows, d), 0),
                                    int(math.log2(tile)))
    tpos = t0 + (lax.broadcasted_iota(jnp.int32, (rows, LANES), 0) & (tile - 1))
    xl = lax.broadcasted_iota(jnp.int32, (rows, d), 1)
    col = lax.broadcasted_iota(jnp.int32, (rows, tile), 1)
    row_tt = lax.broadcasted_iota(jnp.int32, (rows, tile), 0) & (tile - 1)
    lower = col <= row_tt
    upper = col > row_tt
    cend = CMP_STRIDE * lax.broadcasted_iota(jnp.int32, (1, LANES), 1) + (CMP_LEN - 1)
    cmp_ok = tpos >= cend
    any_valid = (tpos >= CMP_LEN - 1).astype(F32)

    def attend(items):
        def scores(x, rb):
            r = slice(rb * ATT_ROW_BLOCK, (rb + 1) * ATT_ROW_BLOCK)
            s = _dot_nt(x[0][r], x[1])
            if x[3] is not None:
                s = jnp.where(x[3][r], s, NEG)
            return s, jnp.max(s, axis=-1, keepdims=True)

        def update(x, rb, s, s_max):
            r = slice(rb * ATT_ROW_BLOCK, (rb + 1) * ATT_ROW_BLOCK)
            g = x[6]
            m_old = x[4][g, r, :]
            m_new = jnp.maximum(m_old, jnp.broadcast_to(s_max, m_old.shape))
            alpha = jnp.exp2(m_old - m_new)
            p = jnp.exp2(s - jnp.concatenate([m_new] * (tile // LANES), axis=1))
            x[5][g, r, :] = jnp.concatenate([alpha, alpha], axis=1) * x[5][g, r, :] + _bdot(p, x[2])
            x[4][g, r, :] = m_new

        pending = None
        for x in items:
            for rb in range(rows // ATT_ROW_BLOCK):
                cur = (x, rb) + scores(x, rb)
                if pending is not None:
                    update(*pending)
                pending = cur
        update(*pending)

    log2e = math.log2(math.e)
    qf = q_ref[...]
    hw = hsum_ref.shape[0]
    sq_hi, sq_lo = _hi_lo(qf * qf)
    ssq = jnp.concatenate(
        [jnp.dot(sq_hi[:, j:j + hw], hsum_ref[...], preferred_element_type=F32)
         + jnp.dot(sq_lo[:, j:j + hw], hsum_ref[...], preferred_element_type=F32)
         for j in range(0, NSA_W, hw)], axis=1)
    qn = qf * lax.rsqrt(ssq * (1.0 / d) + EPS) * (qn_ref[...] * (log2e * d ** -0.5))
    q64, q_x, q_aug = [], [], []
    for g in groups:
        slopes = [log2e * 2.0 ** (-8.0 * (g * rep + r + 1) / NSA_HEADS) for r in range(rep)]
        sl = jnp.full((rows, d), slopes[rep - 1], F32)
        for r in range(rep - 2, -1, -1):
            sl = jnp.where(row_r == r, slopes[r], sl)
        sl_hi = sl.astype(BF16).astype(F32)
        q_x.append(jnp.where((xl == X_SLOPE) | (xl == X_SLOPE + 2), sl_hi,
                             jnp.where((xl == X_SLOPE + 1) | (xl == X_SLOPE + 3), sl - sl_hi, 0.0)))
        q64.append(jnp.concatenate(
            [qn[:, (g * rep + r) * d:(g * rep + r + 1) * d] for r in range(rep)], axis=0))
        q_aug.append(jnp.concatenate([q64[g], q_x[g]], axis=1).astype(BF16))

    s_c = [jnp.where(cmp_ok, _dot_nt(q_aug[g], kca[g]), NEG) for g in groups]
    e_c = [jnp.exp2(s - jnp.max(s, axis=-1, keepdims=True)) for s in s_c]
    pv_c = [_bdot(e_c[g], vca[g]) for g in groups]
    w_c = [any_valid / pv[:, LANES:] for pv in pv_c]
    p_c = [e_c[g] * w_c[g] for g in groups]
    o_cmp = [pv_c[g][:, :LANES] * w_c[g] for g in groups]
    p_sum = [sum(p[r * tile:(r + 1) * tile] for r in range(rep)) for p in p_c]

    jj = lax.broadcasted_iota(jnp.int32, (X_SLOPE, tile), 0)
    blk_t = lax.shift_right_logical(t0 + lax.broadcasted_iota(jnp.int32, (1, tile), 1), int(math.log2(SLC_LEN)))
    imp = [sum(_dot_nt(ovt_ref[...], t) for t in _split3(ps))[:X_SLOPE] for ps in p_sum]
    imp = [jnp.where(jj <= blk_t, x, NEG) for x in imp]
    imp = [jnp.where((jj == blk_t) | (jj == 0), FORCE, x) for x in imp]
    imp = [jnp.where(jj < n_sel, x, -3e38) for x in imp]
    rank = [jnp.zeros((X_SLOPE, tile), F32) for _ in groups]
    for j in range(n_sel):
        for g in groups:
            row = imp[g][j:j + 1, :]
            ge = jnp.where(row >= imp[g], 1.0, 0.0)
            gt = jnp.where(row > imp[g], 1.0, 0.0)
            rank[g] = rank[g] + jnp.where(jj > j, ge, gt)
    k_sel = float(min(SLC_TOPK, n_sel))
    bias_t = [jnp.where((rk < k_sel) | (jj >= n_sel), 0.0, NEG) for rk in rank]
    zero_rows = jnp.zeros((LANES - X_SLOPE, tile), F32)
    sel_bias = [jnp.concatenate([bt, zero_rows], axis=0).T[:, :d] for bt in bias_t]
    q_sel = [jnp.concatenate([q64[g], q_x[g] + jnp.concatenate([sel_bias[g]] * rep, axis=0)],
                             axis=1).astype(BF16) for g in groups]

    for g in groups:
        m_slc[g] = jnp.full((rows, LANES), -jnp.inf, F32)
        m_win[g] = jnp.full((rows, LANES), -jnp.inf, F32)
        acc_slc[g] = jnp.zeros((rows, 2 * LANES), F32)
        acc_win[g] = jnp.zeros((rows, 2 * LANES), F32)

    r_diag = pl.ds(pl.multiple_of(t0, tile), tile)
    attend([(q_sel[g], ksa[g, r_diag, :], vsa[g, r_diag, :], lower, m_slc, acc_slc, g) for g in groups])
    attend([(q_aug[g], kwa[g, r_diag, :], vwa[g, r_diag, :], lower, m_win, acc_win, g) for g in groups])

    @pl.when(it > 0)
    def _():
        r_prev = pl.ds(pl.multiple_of(t0 - tile, tile), tile)
        attend([(q_aug[g], kwa[g, r_prev, :], vwa[g, r_prev, :], upper, m_win, acc_win, g) for g in groups])

    def slc_tile(kt, carry):
        r = pl.ds(pl.multiple_of(kt * tile, tile), tile)
        attend([(q_sel[g], ksa[g, r, :], vsa[g, r, :], None, m_slc, acc_slc, g) for g in groups])
        return carry

    lax.fori_loop(0, it, slc_tile, 0)

    g_hi, g_lo = _hi_lo(jax.nn.sigmoid(sm_ref[...]))
    gate_w = (jnp.dot(g_hi, gsel_ref[...], preferred_element_type=F32)
              + jnp.dot(g_lo, gsel_ref[...], preferred_element_type=F32))
    low_half = lax.broadcasted_iota(jnp.int32, (tile, LANES), 1) < d
    for g in groups:
        o_slc = acc_slc[g][:, :LANES] / acc_slc[g][:, LANES:]
        o_win = acc_win[g][:, :LANES] / acc_win[g][:, LANES:]
        for pr in range(rep // 2):
            pair = g * (rep // 2) + pr
            r_even = slice(2 * pr * tile, (2 * pr + 1) * tile)
            r_odd = slice((2 * pr + 1) * tile, (2 * pr + 2) * tile)
            merged = 0.0
            for br, o in enumerate((o_cmp[g], o_slc, o_win)):
                c0 = (pair * 3 + br) * LANES
                merged = merged + gate_w[:, c0:c0 + LANES] * jnp.where(low_half, o[r_even], o[r_odd])
            o_ref[:, pair * LANES:(pair + 1) * LANES] = merged


def _gate_spread():
    n_pairs = NSA_HEADS // 2
    c = np.arange(LANES)[:, None]
    col = np.arange(n_pairs * 3 * LANES)[None, :]
    pair, br, n = col // (3 * LANES), (col // LANES) % 3, col % LANES
    head = 2 * pair + (n >= NSA_HEAD_DIM)
    return jnp.asarray((c == SMALL_GATE + 3 * head + br).astype(np.float32), BF16)


def _head_sum(width=2 * LANES):
    i = np.arange(width)
    return jnp.asarray((i[:, None] // NSA_HEAD_DIM == i[None, :] // NSA_HEAD_DIM).astype(np.float32), BF16)


def _nsa(p, kc, vc, q_norm, kn_slc, kn_win, ovt, batch, seq):
    gsel, hsum = _gate_spread(), _head_sum()
    q_gain = jnp.tile(q_norm, (1, NSA_HEADS))
    tile = ATT_TILE
    nt = seq // tile
    rows = NSA_REP * tile
    kv = lambda col: pl.BlockSpec((seq, NSA_KV_W), lambda b, i, col=col: (b, col // 2),
                                  pipeline_mode=pl.Buffered(1))
    full = lambda a: pl.BlockSpec(a.shape, lambda b, i: (0,) * a.ndim)
    cmp_blk = pl.BlockSpec((1,) + kc.shape[1:], lambda b, i: (b, 0, 0, 0))
    aug = lambda n: pltpu.VMEM((NSA_GROUPS, n, LANES), BF16)
    val = lambda n: pltpu.VMEM((NSA_GROUPS, n, 2 * LANES), BF16)
    stat = pltpu.VMEM((NSA_GROUPS, rows, LANES), F32)
    acc = pltpu.VMEM((NSA_GROUPS, rows, 2 * LANES), F32)
    return pl.pallas_call(
        _nsa_kernel,
        grid=(batch, nt),
        in_specs=[pl.BlockSpec((tile, NSA_W), lambda b, i: (b * nt + i, COL_NSQ // 8)),
                  kv(COL_KS), kv(COL_VS), kv(COL_KW), kv(COL_VW),
                  cmp_blk, cmp_blk,
                  pl.BlockSpec((tile, LANES), lambda b, i: (b * nt + i, COL_SMALL)),
                  full(q_gain), full(kn_slc), full(kn_win), full(ovt), full(gsel), full(hsum)],
        out_specs=pl.BlockSpec((tile, NSA_W), lambda b, i: (b * nt + i, 0)),
        out_shape=jax.ShapeDtypeStruct((batch * seq, NSA_W), F32),
        scratch_shapes=[aug(seq), val(seq), aug(seq), val(seq), aug(kc.shape[2]), val(kc.shape[2]),
                        stat, acc, stat, acc],
        compiler_params=_cparams(2),
        name="nsa_attention",
    )(p, p, p, p, p, kc, vc, p, q_gain, kn_slc, kn_win, ovt, gsel, hsum)


def _mix_kernel(x_ref, odn_ref, ons_ref, gdn_ref, gns_ref, wdn_ref, wns_ref, wout_ref, o_ref):
    y_dn = jnp.dot(odn_ref[...].astype(BF16), wdn_ref[...], preferred_element_type=F32)
    y_ns = jnp.dot(ons_ref[...].astype(BF16), wns_ref[...], preferred_element_type=F32)
    mix = jax.nn.sigmoid(gdn_ref[...]) * y_dn + jax.nn.sigmoid(gns_ref[...]) * y_ns
    o_ref[...] = x_ref[...] + jnp.dot(mix.astype(BF16), wout_ref[...], preferred_element_type=F32)


def _mix(x2, o_dn, o_ns, p, w_dn, w_ns, w_out, tm=512):
    t = x2.shape[0]
    row = lambda col: pl.BlockSpec((tm, D_MODEL), lambda i, col=col: (i, col))
    wfull = pl.BlockSpec((D_MODEL, D_MODEL), lambda i: (0, 0))
    return pl.pallas_call(
        _mix_kernel,
        grid=(t // tm,),
        in_specs=[row(0), row(0), row(0), row(COL_GDN // 8), row(COL_GNS // 8), wfull, wfull, wfull],
        out_specs=row(0),
        out_shape=jax.ShapeDtypeStruct((t, D_MODEL), F32),
        compiler_params=_cparams(1),
        name="mix_out",
    )(x2, o_dn, o_ns, p, p, w_dn, w_ns, w_out)


def _ffn_kernel(x_ref, g_ref, wup_ref, wdown_ref, o_ref, h_ref, acc_ref):
    f = pl.program_id(1)

    @pl.when(f == 0)
    def _():
        h_ref[...] = _rms(x_ref[...], g_ref[...]).astype(BF16)
        acc_ref[...] = jnp.zeros_like(acc_ref)

    u = jnp.dot(h_ref[...], wup_ref[...], preferred_element_type=F32)
    u = jnp.square(jnp.maximum(u, 0.0)).astype(BF16)
    acc_ref[...] += jnp.dot(u, wdown_ref[...], preferred_element_type=F32)

    @pl.when(f == pl.num_programs(1) - 1)
    def _():
        o_ref[...] = x_ref[...] + acc_ref[...]


def _ffn(x2, gain, w_up, w_down, tm=1024, tf=1024):
    t = x2.shape[0]
    return pl.pallas_call(
        _ffn_kernel,
        grid=(t // tm, D_FF // tf),
        in_specs=[pl.BlockSpec((tm, D_MODEL), lambda i, f: (i, 0)),
                  pl.BlockSpec((1, D_MODEL), lambda i, f: (0, 0)),
                  pl.BlockSpec((D_MODEL, tf), lambda i, f: (0, f)),
                  pl.BlockSpec((tf, D_MODEL), lambda i, f: (f, 0))],
        out_specs=pl.BlockSpec((tm, D_MODEL), lambda i, f: (i, 0)),
        out_shape=jax.ShapeDtypeStruct((t, D_MODEL), F32),
        scratch_shapes=[pltpu.VMEM((tm, D_MODEL), BF16), pltpu.VMEM((tm, D_MODEL), F32)],
        compiler_params=_cparams(2),
        name="ffn",
    )(x2, gain, w_up, w_down)


def _overlap_matrix_t(seq):
    n_cmp = (seq - CMP_LEN) // CMP_STRIDE + 1
    n_sel = seq // SLC_LEN
    c0 = np.arange(n_cmp)[None, :] * CMP_STRIDE
    j0 = np.arange(n_sel)[:, None] * SLC_LEN
    ov = np.clip(np.minimum(c0 + CMP_LEN, j0 + SLC_LEN) - np.maximum(c0, j0), 0, None) / CMP_LEN
    out = np.zeros((LANES, LANES), np.float32)
    out[:n_sel, :n_cmp] = ov
    return jnp.asarray(out, BF16)


def _regroup_w_in(w):
    o_b = 4 * DN_W
    o_q = o_b + 2 * DN_HEADS
    o_kv = o_q + NSA_W
    o_gate = o_kv + 6 * NSA_KV_W
    o_gdn = o_gate + 3 * NSA_HEADS
    pad = jnp.zeros((w.shape[0], P_WIDTH - w.shape[1]), w.dtype)
    return jnp.concatenate(
        [w[:, :o_b], w[:, o_q:o_kv], w[:, o_gdn:], w[:, o_kv:o_gate], w[:, o_b:o_q],
         w[:, o_gate:o_gdn], pad], axis=1).astype(BF16)


def _blocks16(p, col, batch, seq):
    t = p[:, col * LANES:col * LANES + NSA_KV_W]
    t = t.reshape(batch, seq // CMP_STRIDE, CMP_STRIDE, NSA_GROUPS, NSA_HEAD_DIM)
    t = jnp.transpose(t, (0, 3, 1, 2, 4))
    return t.reshape(batch, NSA_GROUPS, seq // CMP_STRIDE, CMP_STRIDE * NSA_HEAD_DIM)


def kernel(x, norm_mix, w_in, dn_conv, dn_a_log, dn_dt_bias, dn_out_norm, nsa_q_norm, nsa_k_norm_cmp,
           nsa_k_norm_slc, nsa_k_norm_win, cmp_pos_k, cmp_w1_k, cmp_w2_k, cmp_pos_v, cmp_w1_v, cmp_w2_v,
           w_proj_dn, w_proj_nsa, w_out, norm_mlp, w_up, w_down):
    batch, seq, _ = x.shape
    assert seq // CMP_STRIDE == LANES and seq % ATT_TILE == 0 and seq // SLC_LEN <= X_SLOPE
    ovt = _overlap_matrix_t(seq)
    x2 = x.reshape(batch * seq, D_MODEL)
    for l in range(w_in.shape[0]):
        p = _in_proj(x2, norm_mix[l][None], _regroup_w_in(w_in[l]))
        o_dn = _gdn(p, dn_conv[l], dn_a_log[l][None], dn_dt_bias[l][None], dn_out_norm[l][None],
                    batch, seq)
        flat = lambda a: a.reshape(1, -1)
        kc, vc = _compress(_blocks16(p, COL_KC, batch, seq), _blocks16(p, COL_VC, batch, seq),
                           flat(cmp_pos_k[l]), flat(cmp_pos_v[l]),
                           cmp_w1_k[l].astype(BF16), cmp_w2_k[l].astype(BF16),
                           cmp_w1_v[l].astype(BF16), cmp_w2_v[l].astype(BF16),
                           nsa_k_norm_cmp[l][None])
        o_ns = _nsa(p, kc, vc, nsa_q_norm[l][None], nsa_k_norm_slc[l][None], nsa_k_norm_win[l][None],
                    ovt, batch, seq)
        x2 = _mix(x2, o_dn, o_ns, p, w_proj_dn[l].astype(BF16), w_proj_nsa[l].astype(BF16),
                  w_out[l].astype(BF16))
        x2 = _ffn(x2, norm_mlp[l][None], w_up[l].astype(BF16), w_down[l].astype(BF16))
    return x2.reshape(batch, seq, D_MODEL)
```

```python
import functools
import math

import numpy as np
import jax
import jax.numpy as jnp
from jax import lax
from jax.experimental import pallas as pl
from jax.experimental.pallas import tpu as pltpu

F32 = jnp.float32
BF16 = jnp.bfloat16

D_MODEL = 1024
DN_HEADS = 8
DN_HEAD_DIM = 128
DN_W = DN_HEADS * DN_HEAD_DIM
DN_CONV = 4
NSA_HEADS = 16
NSA_GROUPS = 4
NSA_REP = NSA_HEADS // NSA_GROUPS
NSA_HEAD_DIM = 64
NSA_W = NSA_HEADS * NSA_HEAD_DIM
NSA_KV_W = NSA_GROUPS * NSA_HEAD_DIM
CMP_LEN = 32
CMP_STRIDE = 16
CMP_HIDDEN = 2 * NSA_HEAD_DIM
SLC_LEN = 64
SLC_TOPK = 8
WINDOW = 256
D_FF = 4 * D_MODEL
EPS = 1e-6
NEG = -1e30
FORCE = 1e9
LANES = 128

DN_CHUNK = LANES
DN_INTRA_GROUP = 8
ATT_TILE = WINDOW
ATT_ROW_BLOCK = 128

COL_Q, COL_K, COL_V, COL_Z = 0, 8, 16, 24
COL_NSQ, COL_GDN, COL_GNS = 32, 40, 48
COL_KC, COL_VC, COL_KS, COL_VS, COL_KW, COL_VW = 56, 58, 60, 62, 64, 66
COL_SMALL = 68
P_WIDTH = 72 * LANES
SMALL_B, SMALL_A, SMALL_GATE = 0, DN_HEADS, 2 * DN_HEADS

X_SLOPE = SLC_TOPK * 4

VMEM_LIMIT = 56 * 1024 * 1024


def _cparams(n_axes):
    return pltpu.CompilerParams(dimension_semantics=("arbitrary",) * n_axes,
                                vmem_limit_bytes=VMEM_LIMIT)


def _bdot(a, b):
    return jnp.dot(a.astype(BF16), b.astype(BF16), preferred_element_type=F32)


def _dot_nt(a, b):
    return lax.dot_general(a.astype(BF16), b.astype(BF16), (((1,), (1,)), ((), ())),
                           preferred_element_type=F32)


def _dot_tn(a, b):
    return lax.dot_general(a.astype(BF16), b.astype(BF16), (((0,), (0,)), ((), ())),
                           preferred_element_type=F32)


def _split3(a):
    a1 = a.astype(BF16)
    r = a - a1.astype(F32)
    a2 = r.astype(BF16)
    a3 = (r - a2.astype(F32)).astype(BF16)
    return a1, a2, a3


def _hi_lo(a):
    hi = a.astype(BF16)
    return hi, (a - hi.astype(F32)).astype(BF16)


def _dot_hl(x, y):
    (xh, xl), (yh, yl) = x, y
    return (jnp.dot(jnp.concatenate([xh, xl], axis=1), jnp.concatenate([yh, yh], axis=0),
                    preferred_element_type=F32)
            + jnp.dot(xh, yl, preferred_element_type=F32))


def _dot_sel_rhs(a, sel):
    return sum(jnp.dot(t, sel, preferred_element_type=F32) for t in _split3(a))


def _dot_sel_lhs(sel, b):
    return sum(jnp.dot(sel, t, preferred_element_type=F32) for t in _split3(b))


def _rms(x, gain):
    return x * lax.rsqrt(jnp.mean(x * x, axis=-1, keepdims=True) + EPS) * gain


def _inproj_kernel(x_ref, g_ref, w_ref, ws_ref, o_ref, os_ref, h_ref):
    @pl.when(pl.program_id(1) == 0)
    def _():
        h_ref[...] = _rms(x_ref[...], g_ref[...]).astype(BF16)
        os_ref[...] = jnp.dot(h_ref[...], ws_ref[...], preferred_element_type=F32)

    o_ref[...] = jnp.dot(h_ref[...], w_ref[...], preferred_element_type=F32).astype(BF16)


def _in_proj(x2, gain, w_perm, tm=1024, tn=1536):
    t = x2.shape[0]
    return pl.pallas_call(
        _inproj_kernel,
        grid=(t // tm, P_WIDTH // tn),
        in_specs=[pl.BlockSpec((tm, D_MODEL), lambda i, j: (i, 0)),
                  pl.BlockSpec((1, D_MODEL), lambda i, j: (0, 0)),
                  pl.BlockSpec((D_MODEL, tn), lambda i, j: (0, j)),
                  pl.BlockSpec((D_MODEL, LANES), lambda i, j: (0, COL_SMALL))],
        out_specs=[pl.BlockSpec((tm, tn), lambda i, j: (i, j)),
                   pl.BlockSpec((tm, LANES), lambda i, j: (i, 0))],
        out_shape=[jax.ShapeDtypeStruct((t, P_WIDTH), BF16), jax.ShapeDtypeStruct((t, LANES), F32)],
        scratch_shapes=[pltpu.VMEM((tm, D_MODEL), BF16)],
        compiler_params=_cparams(2),
        name="in_proj",
    )(x2, gain, w_perm, w_perm)


def _conv_silu(x, w):
    rows = lax.broadcasted_iota(jnp.int32, x.shape, 0)
    y = x * w[DN_CONV - 1:DN_CONV, :]
    for j in range(DN_CONV - 1):
        sh = DN_CONV - 1 - j
        xs = jnp.where(rows >= sh, pltpu.roll(x, sh, axis=0), 0.0)
        y = y + xs * w[j:j + 1, :]
    return y * jax.nn.sigmoid(y)


def _gdn_kernel(alog_ref, dtb_ref, q_ref, k_ref, v_ref, z_ref, sm_ref, cq_ref, ck_ref, cv_ref,
                onorm_ref, o_ref, qs, ks, vs, gs, bs, us, ws, qks, os_, gls):
    h = pl.program_id(1)
    s_len = q_ref.shape[0]
    c = DN_CHUNK
    n_chunks = s_len // c

    q = _conv_silu(q_ref[...].astype(F32), cq_ref[...])
    k = _conv_silu(k_ref[...].astype(F32), ck_ref[...])
    v = _conv_silu(v_ref[...].astype(F32), cv_ref[...])
    qs[...] = q * lax.rsqrt(jnp.sum(q * q, axis=-1, keepdims=True) + EPS) * (DN_HEAD_DIM ** -0.5)
    ks[...] = k * lax.rsqrt(jnp.sum(k * k, axis=-1, keepdims=True) + EPS)
    vs[...] = v

    kk = lax.broadcasted_iota(jnp.int32, (LANES, 2 * LANES), 0)
    nn = lax.broadcasted_iota(jnp.int32, (LANES, 2 * LANES), 1)
    onehot = (kk == jnp.where(nn < LANES, SMALL_B + h, SMALL_A + h)).astype(BF16)
    ba = _dot_sel_rhs(sm_ref[...], onehot)
    neg_a = -jnp.exp(jnp.full((1, LANES), alog_ref[0, h], F32))
    dtb = jnp.full((1, LANES), dtb_ref[0, h], F32)
    bs[...] = jax.nn.sigmoid(ba[:, :LANES])
    gs[...] = neg_a * jax.nn.softplus(ba[:, LANES:] + dtb)

    ri = lax.broadcasted_iota(jnp.int32, (c, c), 0)
    ci = lax.broadcasted_iota(jnp.int32, (c, c), 1)
    causal = ri >= ci
    strict = ri > ci
    eye = (ri == ci).astype(F32)
    ltri = causal.astype(BF16)

    def intra(ig, carry):
        grp = range(DN_INTRA_GROUP)
        rs = [pl.ds(pl.multiple_of((ig * DN_INTRA_GROUP + j) * c, c), c) for j in grp]
        qc, kc, vc, bc = ([ref[r, :] for r in rs] for ref in (qs, ks, vs, bs))
        gcum = [_dot_sel_lhs(ltri, gs[r, :]) for r in rs]
        decay = [jnp.exp(jnp.where(causal, g - g.T, -jnp.inf)) for g in gcum]
        kb = [kc[j] * bc[j] for j in grp]
        a = [jnp.where(strict, _dot_nt(kb[j], kc[j]) * decay[j], 0.0) for j in grp]
        tinv = [eye - a[j] for j in grp]
        pw = [_hi_lo(x) for x in a]
        for _ in range(int(math.log2(c)) - 1):
            pw = [_hi_lo(_dot_hl(x, x)) for x in pw]
            tinv = [tinv[j] + _dot_hl(_hi_lo(tinv[j]), pw[j]) for j in grp]
        eg = [jnp.exp(g) for g in gcum]
        uw = [_bdot(tinv[j], jnp.concatenate([vc[j] * bc[j], kb[j] * eg[j]], axis=1)) for j in grp]
        qk = [_dot_nt(qc[j], kc[j]) * decay[j] for j in grp]
        glast = [g[c - 1:c, :] for g in gcum]
        k_end = [kc[j] * jnp.exp(glast[j] - gcum[j]) for j in grp]
        qk_uw = [_bdot(qk[j], uw[j]) for j in grp]
        ke_uw = [_dot_tn(k_end[j], uw[j]) for j in grp]
        for j in grp:
            r = rs[j]
            qs[r, :] = qc[j] * eg[j] - qk_uw[j][:, LANES:]
            qks[r, :] = qk_uw[j][:, :LANES]
            ws[r, :] = ke_uw[j][:, LANES:]
            us[r, :] = ke_uw[j][:, :LANES]
            gls[pl.ds(ig * DN_INTRA_GROUP + j, 1), :] = jnp.exp(glast[j])
        return carry

    lax.fori_loop(0, n_chunks // DN_INTRA_GROUP, intra, 0)

    def inter(ic, state):
        r = pl.ds(pl.multiple_of(ic * c, c), c)
        os_[r, :] = _bdot(qs[r, :], state) + qks[r, :]
        return state * gls[pl.ds(ic, 1), :] - _bdot(ws[r, :], state) + us[r, :]

    lax.fori_loop(0, n_chunks, inter, jnp.zeros((DN_HEAD_DIM, DN_HEAD_DIM), F32))

    z = z_ref[...].astype(F32)
    o_ref[...] = (_rms(os_[...], onorm_ref[...]) * (z * jax.nn.sigmoid(z))).astype(BF16)


def _gdn(p, p_small, conv_w, a_log, dt_bias, out_norm, batch, seq):
    row_blk = lambda col0: pl.BlockSpec((seq, LANES), lambda b, h, col0=col0: (b, col0 + h))
    conv_blk = lambda col0: pl.BlockSpec((DN_CONV, LANES), lambda b, h, col0=col0: (0, col0 + h))
    smem = pl.BlockSpec(memory_space=pltpu.SMEM)
    return pl.pallas_call(
        _gdn_kernel,
        grid=(batch, DN_HEADS),
        in_specs=[smem, smem,
                  row_blk(COL_Q), row_blk(COL_K), row_blk(COL_V), row_blk(COL_Z),
                  pl.BlockSpec((seq, LANES), lambda b, h: (b, 0)),
                  conv_blk(COL_Q), conv_blk(COL_K), conv_blk(COL_V),
                  pl.BlockSpec((1, LANES), lambda b, h: (0, 0))],
        out_specs=pl.BlockSpec((seq, LANES), lambda b, h: (b, h)),
        out_shape=jax.ShapeDtypeStruct((batch * seq, DN_W), BF16),
        scratch_shapes=([pltpu.VMEM((seq, LANES), F32) for _ in range(9)]
                        + [pltpu.VMEM((seq // DN_CHUNK, LANES), F32)]),
        compiler_params=_cparams(2),
        name="gdn",
    )(a_log, dt_bias, p, p, p, p, p_small, conv_w, conv_w, conv_w, out_norm)


def _gelu_tanh(x):
    return 0.5 * x * (1.0 + jnp.tanh(math.sqrt(2.0 / math.pi) * (x + 0.044715 * (x * x * x))))


def _compress_one(t2, pos, w1, w2):
    n = t2.shape[0]
    half = t2.shape[1]
    y1 = _bdot(t2 + pos[:, :half], w1[:half, :])
    y2 = _bdot(t2 + pos[:, half:], w1[half:, :])
    hid = _gelu_tanh(y1 + pltpu.roll(y2, n - 1, axis=0))
    return _bdot(hid, w2)


def _compress_kernel(tk_ref, tv_ref, pk_ref, pv_ref, w1k_ref, w2k_ref, w1v_ref, w2v_ref, kn_ref,
                     kc_ref, vc_ref):
    kc = _compress_one(tk_ref[0, 0].astype(F32), pk_ref[...], w1k_ref[...], w2k_ref[...])
    kc_ref[0, 0] = _rms(kc, kn_ref[...])
    vc_ref[0, 0] = _compress_one(tv_ref[0, 0].astype(F32), pv_ref[...], w1v_ref[...], w2v_ref[...])


def _compress(t2k, t2v, pos_k, pos_v, w1k, w2k, w1v, w2v, kn_cmp):
    batch, groups, n, width = t2k.shape
    d = NSA_HEAD_DIM
    tok = pl.BlockSpec((1, 1, n, width), lambda b, g: (b, g, 0, 0))
    full = lambda a: pl.BlockSpec(a.shape, lambda b, g: (0,) * a.ndim)
    out = pl.BlockSpec((1, 1, n, d), lambda b, g: (b, g, 0, 0))
    return pl.pallas_call(
        _compress_kernel,
        grid=(batch, groups),
        in_specs=[tok, tok, full(pos_k), full(pos_v), full(w1k), full(w2k), full(w1v), full(w2v),
                  full(kn_cmp)],
        out_specs=[out, out],
        out_shape=[jax.ShapeDtypeStruct((batch, groups, n, d), F32)] * 2,
        compiler_params=_cparams(2),
        name="nsa_compress",
    )(t2k, t2v, pos_k, pos_v, w1k, w2k, w1v, w2v, kn_cmp)


def _nsa_kernel(q_ref, ks_ref, vs_ref, kw_ref, vw_ref, kc_ref, vc_ref, sm_ref, qn_ref, kns_ref,
                knw_ref, ovt_ref, gsel_ref, hsum_ref, o_ref, ksa, vsa, kwa, vwa, kca, vca, m_slc, acc_slc,
                m_win, acc_win):
    it = pl.program_id(1)
    tile = ATT_TILE
    s_len = ks_ref.shape[0]
    d = NSA_HEAD_DIM
    rep = NSA_REP
    rows = rep * tile
    n_sel = s_len // SLC_LEN
    n_cend = kc_ref.shape[2]

    @pl.when(it == 0)
    def _():
        rowi = lax.broadcasted_iota(jnp.int32, (s_len, d), 0)
        xl = lax.broadcasted_iota(jnp.int32, (s_len, d), 1)
        in_tile = (rowi & (tile - 1)).astype(F32)
        tile_start = (rowi - (rowi & (tile - 1))).astype(F32)
        pos_cols = jnp.where((xl == X_SLOPE) | (xl == X_SLOPE + 1), in_tile,
                             jnp.where((xl == X_SLOPE + 2) | (xl == X_SLOPE + 3), tile_start, 0.0))
        blk_cols = jnp.where(xl == lax.shift_right_logical(rowi, int(math.log2(SLC_LEN))), 1.0, 0.0)
        ci = lax.broadcasted_iota(jnp.int32, (n_cend, d), 0)
        cl = lax.broadcasted_iota(jnp.int32, (n_cend, d), 1)
        cend_cols = jnp.where((cl == X_SLOPE) | (cl == X_SLOPE + 1), (CMP_STRIDE * ci).astype(F32), 0.0)
        ones_s = jnp.ones((s_len, LANES), F32)
        ones_c = jnp.ones((n_cend, LANES), F32)
        for g in range(NSA_GROUPS):
            sl = slice(g * d, (g + 1) * d)
            ksa[g] = jnp.concatenate([_rms(ks_ref[:, sl].astype(F32), kns_ref[...]), pos_cols + blk_cols],
                                     axis=1).astype(BF16)
            kwa[g] = jnp.concatenate([_rms(kw_ref[:, sl].astype(F32), knw_ref[...]), pos_cols],
                                     axis=1).astype(BF16)
            kca[g] = jnp.concatenate([kc_ref[0, g], cend_cols], axis=1).astype(BF16)
            vs_g, vw_g = vs_ref[:, sl].astype(F32), vw_ref[:, sl].astype(F32)
            vsa[g] = jnp.concatenate([vs_g, vs_g, ones_s], axis=1).astype(BF16)
            vwa[g] = jnp.concatenate([vw_g, vw_g, ones_s], axis=1).astype(BF16)
            vca[g] = jnp.concatenate([vc_ref[0, g], vc_ref[0, g], ones_c], axis=1).astype(BF16)

    t0 = it * tile
    groups = range(NSA_GROUPS)
    row_r =lax.shift_right_logical(lax.broadcasted_iota(jnp.int32, (rows, d), 0),
                                    int(math.log2(tile)))
    tpos = t0 + (lax.broadcasted_iota(jnp.int32, (rows, LANES), 0) & (tile - 1))
    xl = lax.broadcasted_iota(jnp.int32, (rows, d), 1)
    col = lax.broadcasted_iota(jnp.int32, (rows, tile), 1)
    row_tt = lax.broadcasted_iota(jnp.int32, (rows, tile), 0) & (tile - 1)
    lower = col <= row_tt
    upper = col > row_tt
    cend = CMP_STRIDE * lax.broadcasted_iota(jnp.int32, (1, LANES), 1) + (CMP_LEN - 1)
    cmp_ok = tpos >= cend
    any_valid = (tpos >= CMP_LEN - 1).astype(F32)

    def attend(items):
        def scores(x, rb):
            r = slice(rb * ATT_ROW_BLOCK, (rb + 1) * ATT_ROW_BLOCK)
            s = _dot_nt(x[0][r], x[1])
            if x[3] is not None:
                s = jnp.where(x[3][r], s, NEG)
            return s, jnp.max(s, axis=-1, keepdims=True)

        def update(x, rb, s, s_max):
            r = slice(rb * ATT_ROW_BLOCK, (rb + 1) * ATT_ROW_BLOCK)
            g = x[6]
            m_old = x[4][g, r, :]
            m_new = jnp.maximum(m_old, jnp.broadcast_to(s_max, m_old.shape))
            alpha = jnp.exp2(m_old - m_new)
            p = jnp.exp2(s - jnp.concatenate([m_new] * (tile // LANES), axis=1))
            x[5][g, r, :] = jnp.concatenate([alpha, alpha], axis=1) * x[5][g, r, :] + _bdot(p, x[2])
            x[4][g, r, :] = m_new

        pending = None
        for x in items:
            for rb in range(rows // ATT_ROW_BLOCK):
                cur = (x, rb) + scores(x, rb)
                if pending is not None:
                    update(*pending)
                pending = cur
        update(*pending)

    log2e = math.log2(math.e)
    qf = q_ref[...].astype(F32)
    hw = hsum_ref.shape[0]
    sq_hi, sq_lo = _hi_lo(qf * qf)
    ssq = jnp.concatenate(
        [jnp.dot(sq_hi[:, j:j + hw], hsum_ref[...], preferred_element_type=F32)
         + jnp.dot(sq_lo[:, j:j + hw], hsum_ref[...], preferred_element_type=F32)
         for j in range(0, NSA_W, hw)], axis=1)
    qn = qf * lax.rsqrt(ssq * (1.0 / d) + EPS) * (qn_ref[...] * (log2e * d ** -0.5))
    q64, q_x, q_aug = [], [], []
    for g in groups:
        slopes = [log2e * 2.0 ** (-8.0 * (g * rep + r + 1) / NSA_HEADS) for r in range(rep)]
        sl = jnp.full((rows, d), slopes[rep - 1], F32)
        for r in range(rep - 2, -1, -1):
            sl = jnp.where(row_r == r, slopes[r], sl)
        sl_hi = sl.astype(BF16).astype(F32)
        q_x.append(jnp.where((xl == X_SLOPE) | (xl == X_SLOPE + 2), sl_hi,
                             jnp.where((xl == X_SLOPE + 1) | (xl == X_SLOPE + 3), sl - sl_hi, 0.0)))
        q64.append(jnp.concatenate(
            [qn[:, (g * rep + r) * d:(g * rep + r + 1) * d] for r in range(rep)], axis=0))
        q_aug.append(jnp.concatenate([q64[g], q_x[g]], axis=1).astype(BF16))

    s_c = [jnp.where(cmp_ok, _dot_nt(q_aug[g], kca[g]), NEG) for g in groups]
    e_c = [jnp.exp2(s - jnp.max(s, axis=-1, keepdims=True)) for s in s_c]
    pv_c = [_bdot(e_c[g], vca[g]) for g in groups]
    w_c = [any_valid / pv[:, LANES:] for pv in pv_c]
    p_c = [e_c[g] * w_c[g] for g in groups]
    o_cmp = [pv_c[g][:, :LANES] * w_c[g] for g in groups]
    p_sum = [sum(p[r * tile:(r + 1) * tile] for r in range(rep)) for p in p_c]

    jj = lax.broadcasted_iota(jnp.int32, (X_SLOPE, tile), 0)
    blk_t = lax.shift_right_logical(t0 + lax.broadcasted_iota(jnp.int32, (1, tile), 1), int(math.log2(SLC_LEN)))
    imp = [sum(_dot_nt(ovt_ref[...], t) for t in _split3(ps))[:X_SLOPE] for ps in p_sum]
    imp = [jnp.where(jj <= blk_t, x, NEG) for x in imp]
    imp = [jnp.where((jj == blk_t) | (jj == 0), FORCE, x) for x in imp]
    imp = [jnp.where(jj < n_sel, x, -3e38) for x in imp]
    rank = [jnp.zeros((X_SLOPE, tile), F32) for _ in groups]
    for j in range(n_sel):
        for g in groups:
            row = imp[g][j:j + 1, :]
            ge = jnp.where(row >= imp[g], 1.0, 0.0)
            gt = jnp.where(row > imp[g], 1.0, 0.0)
            rank[g] = rank[g] + jnp.where(jj > j, ge, gt)
    k_sel = float(min(SLC_TOPK, n_sel))
    bias_t = [jnp.where((rk < k_sel) | (jj >= n_sel), 0.0, NEG) for rk in rank]
    zero_rows = jnp.zeros((LANES - X_SLOPE, tile), F32)
    sel_bias = [jnp.concatenate([bt, zero_rows], axis=0).T[:, :d] for bt in bias_t]
    q_sel = [jnp.concatenate([q64[g], q_x[g] + jnp.concatenate([sel_bias[g]] * rep, axis=0)],
                             axis=1).astype(BF16) for g in groups]

    for g in groups:
        m_slc[g] = jnp.full((rows, LANES), -jnp.inf, F32)
        m_win[g] = jnp.full((rows, LANES), -jnp.inf, F32)
        acc_slc[g] = jnp.zeros((rows, 2 * LANES), F32)
        acc_win[g] = jnp.zeros((rows, 2 * LANES), F32)

    r_diag = pl.ds(pl.multiple_of(t0, tile), tile)
    attend([(q_sel[g], ksa[g, r_diag, :], vsa[g, r_diag, :], lower, m_slc, acc_slc, g) for g in groups])
    attend([(q_aug[g], kwa[g, r_diag, :], vwa[g, r_diag, :], lower, m_win, acc_win, g) for g in groups])

    @pl.when(it > 0)
    def _():
        r_prev = pl.ds(pl.multiple_of(t0 - tile, tile), tile)
        attend([(q_aug[g], kwa[g, r_prev, :], vwa[g, r_prev, :], upper, m_win, acc_win, g) for g in groups])

    def slc_tile(kt, carry):
        r = pl.ds(pl.multiple_of(kt * tile, tile), tile)
        attend([(q_sel[g], ksa[g, r, :], vsa[g, r, :], None, m_slc, acc_slc, g) for g in groups])
        return carry

    lax.fori_loop(0, it, slc_tile, 0)

    g_hi, g_lo = _hi_lo(jax.nn.sigmoid(sm_ref[...]))
    gate_w = (jnp.dot(g_hi, gsel_ref[...], preferred_element_type=F32)
              + jnp.dot(g_lo, gsel_ref[...], preferred_element_type=F32))
    low_half = lax.broadcasted_iota(jnp.int32, (tile, LANES), 1) < d
    for g in groups:
        o_slc = acc_slc[g][:, :LANES] / acc_slc[g][:, LANES:]
        o_win = acc_win[g][:, :LANES] / acc_win[g][:, LANES:]
        for pr in range(rep // 2):
            pair = g * (rep // 2) + pr
            r_even = slice(2 * pr * tile, (2 * pr + 1) * tile)
            r_odd = slice((2 * pr + 1) * tile, (2 * pr + 2) * tile)
            merged = 0.0
            for br, o in enumerate((o_cmp[g], o_slc, o_win)):
                c0 = (pair * 3 + br) * LANES
                merged = merged + gate_w[:, c0:c0 + LANES] * jnp.where(low_half, o[r_even], o[r_odd])
            o_ref[:, pair * LANES:(pair + 1) * LANES] = merged.astype(BF16)


def _gate_spread():
    n_pairs = NSA_HEADS // 2
    c = np.arange(LANES)[:, None]
    col = np.arange(n_pairs * 3 * LANES)[None, :]
    pair, br, n = col // (3 * LANES), (col // LANES) % 3, col % LANES
    head = 2 * pair + (n >= NSA_HEAD_DIM)
    return jnp.asarray((c == SMALL_GATE + 3 * head + br).astype(np.float32), BF16)


def _head_sum(width=2 * LANES):
    i = np.arange(width)
    return jnp.asarray((i[:, None] // NSA_HEAD_DIM == i[None, :] // NSA_HEAD_DIM).astype(np.float32), BF16)


def _nsa(p, p_small, kc, vc, q_norm, kn_slc, kn_win, ovt, batch, seq):
    gsel, hsum = _gate_spread(), _head_sum()
    q_gain = jnp.tile(q_norm, (1, NSA_HEADS))
    tile = ATT_TILE
    nt = seq // tile
    rows = NSA_REP * tile
    kv = lambda col: pl.BlockSpec((seq, NSA_KV_W), lambda b, i, col=col: (b, col // 2),
                                  pipeline_mode=pl.Buffered(1))
    full = lambda a: pl.BlockSpec(a.shape, lambda b, i: (0,) * a.ndim)
    cmp_blk = pl.BlockSpec((1,) + kc.shape[1:], lambda b, i: (b, 0, 0, 0))
    aug = lambda n: pltpu.VMEM((NSA_GROUPS, n, LANES), BF16)
    val = lambda n: pltpu.VMEM((NSA_GROUPS, n, 2 * LANES), BF16)
    stat = pltpu.VMEM((NSA_GROUPS, rows, LANES), F32)
    acc = pltpu.VMEM((NSA_GROUPS, rows, 2 * LANES), F32)
    return pl.pallas_call(
        _nsa_kernel,
        grid=(batch, nt),
        in_specs=[pl.BlockSpec((tile, NSA_W), lambda b, i: (b * nt + i, COL_NSQ // 8)),
                  kv(COL_KS), kv(COL_VS), kv(COL_KW), kv(COL_VW),
                  cmp_blk, cmp_blk,
                  pl.BlockSpec((tile, LANES), lambda b, i: (b * nt + i, 0)),
                  full(q_gain), full(kn_slc), full(kn_win), full(ovt), full(gsel), full(hsum)],
        out_specs=pl.BlockSpec((tile, NSA_W), lambda b, i: (b * nt + i, 0)),
        out_shape=jax.ShapeDtypeStruct((batch * seq, NSA_W), BF16),
        scratch_shapes=[aug(seq), val(seq), aug(seq), val(seq), aug(kc.shape[2]), val(kc.shape[2]),
                        stat, acc, stat, acc],
        compiler_params=_cparams(2),
        name="nsa_attention",
    )(p, p, p, p, p, kc, vc, p_small, q_gain, kn_slc, kn_win, ovt, gsel, hsum)


def _mix_kernel(x_ref, odn_ref, ons_ref, gdn_ref, gns_ref, wdn_ref, wns_ref, wout_ref, o_ref):
    y_dn = jnp.dot(odn_ref[...], wdn_ref[...], preferred_element_type=F32)
    y_ns = jnp.dot(ons_ref[...], wns_ref[...], preferred_element_type=F32)
    mix = (jax.nn.sigmoid(gdn_ref[...].astype(F32)) * y_dn
           + jax.nn.sigmoid(gns_ref[...].astype(F32)) * y_ns)
    o_ref[...] = x_ref[...] + jnp.dot(mix.astype(BF16), wout_ref[...], preferred_element_type=F32)


def _mix(x2, o_dn, o_ns, p, w_dn, w_ns, w_out, tm=512):
    t = x2.shape[0]
    row = lambda col: pl.BlockSpec((tm, D_MODEL), lambda i, col=col: (i, col))
    wfull = pl.BlockSpec((D_MODEL, D_MODEL), lambda i: (0, 0))
    return pl.pallas_call(
        _mix_kernel,
        grid=(t // tm,),
        in_specs=[row(0), row(0), row(0), row(COL_GDN // 8), row(COL_GNS // 8), wfull, wfull, wfull],
        out_specs=row(0),
        out_shape=jax.ShapeDtypeStruct((t, D_MODEL), F32),
        compiler_params=_cparams(1),
        name="mix_out",
    )(x2, o_dn, o_ns, p, p, w_dn, w_ns, w_out)


def _ffn_kernel(x_ref, g_ref, wup_ref, wdown_ref, o_ref, h_ref, acc_ref):
    f = pl.program_id(1)

    @pl.when(f == 0)
    def _():
        h_ref[...] = _rms(x_ref[...], g_ref[...]).astype(BF16)
        acc_ref[...] = jnp.zeros_like(acc_ref)

    u = jnp.dot(h_ref[...], wup_ref[...], preferred_element_type=F32)
    u = jnp.square(jnp.maximum(u, 0.0)).astype(BF16)
    acc_ref[...] += jnp.dot(u, wdown_ref[...], preferred_element_type=F32)

    @pl.when(f == pl.num_programs(1) - 1)
    def _():
        o_ref[...] = x_ref[...] + acc_ref[...]


def _ffn(x2, gain, w_up, w_down, tm=1024, tf=1024):
    t = x2.shape[0]
    return pl.pallas_call(
        _ffn_kernel,
        grid=(t // tm, D_FF // tf),
        in_specs=[pl.BlockSpec((tm, D_MODEL), lambda i, f: (i, 0)),
                  pl.BlockSpec((1, D_MODEL), lambda i, f: (0, 0)),
                  pl.BlockSpec((D_MODEL, tf), lambda i, f: (0, f)),
                  pl.BlockSpec((tf, D_MODEL), lambda i, f: (f, 0))],
        out_specs=pl.BlockSpec((tm, D_MODEL), lambda i, f: (i, 0)),
        out_shape=jax.ShapeDtypeStruct((t, D_MODEL), F32),
        scratch_shapes=[pltpu.VMEM((tm, D_MODEL), BF16), pltpu.VMEM((tm, D_MODEL), F32)],
        compiler_params=_cparams(2),
        name="ffn",
    )(x2, gain, w_up, w_down)


def _overlap_matrix_t(seq):
    n_cmp = (seq - CMP_LEN) // CMP_STRIDE + 1
    n_sel = seq // SLC_LEN
    c0 = np.arange(n_cmp)[None, :] * CMP_STRIDE
    j0 = np.arange(n_sel)[:, None] * SLC_LEN
    ov = np.clip(np.minimum(c0 + CMP_LEN, j0 + SLC_LEN) - np.maximum(c0, j0), 0, None) / CMP_LEN
    out = np.zeros((LANES, LANES), np.float32)
    out[:n_sel, :n_cmp] = ov
    return jnp.asarray(out, BF16)


def _regroup_w_in(w):
    o_b = 4 * DN_W
    o_q = o_b + 2 * DN_HEADS
    o_kv = o_q + NSA_W
    o_gate = o_kv + 6 * NSA_KV_W
    o_gdn = o_gate + 3 * NSA_HEADS
    pad = jnp.zeros((w.shape[0], P_WIDTH - w.shape[1]), w.dtype)
    return jnp.concatenate(
        [w[:, :o_b], w[:, o_q:o_kv], w[:, o_gdn:], w[:, o_kv:o_gate], w[:, o_b:o_q],
         w[:, o_gate:o_gdn], pad], axis=1).astype(BF16)


def _blocks16(p, col, batch, seq):
    t = p[:, col * LANES:col * LANES + NSA_KV_W]
    t = t.reshape(batch, seq // CMP_STRIDE, CMP_STRIDE, NSA_GROUPS, NSA_HEAD_DIM)
    t = jnp.transpose(t, (0, 3, 1, 2, 4))
    return t.reshape(batch, NSA_GROUPS, seq // CMP_STRIDE, CMP_STRIDE * NSA_HEAD_DIM)


def kernel(x, norm_mix, w_in, dn_conv, dn_a_log, dn_dt_bias, dn_out_norm, nsa_q_norm, nsa_k_norm_cmp,
           nsa_k_norm_slc, nsa_k_norm_win, cmp_pos_k, cmp_w1_k, cmp_w2_k, cmp_pos_v, cmp_w1_v, cmp_w2_v,
           w_proj_dn, w_proj_nsa, w_out, norm_mlp, w_up, w_down):
    batch, seq, _ = x.shape
    assert seq // CMP_STRIDE == LANES and seq % ATT_TILE == 0 and seq // SLC_LEN <= X_SLOPE
    ovt = _overlap_matrix_t(seq)
    x2 = x.reshape(batch * seq, D_MODEL)
    for l in range(w_in.shape[0]):
        p, p_small = _in_proj(x2, norm_mix[l][None], _regroup_w_in(w_in[l]))
        o_dn = _gdn(p, p_small, dn_conv[l], dn_a_log[l][None], dn_dt_bias[l][None], dn_out_norm[l][None],
                    batch, seq)
        flat = lambda a: a.reshape(1, -1)
        kc, vc = _compress(_blocks16(p, COL_KC, batch, seq), _blocks16(p, COL_VC, batch, seq),
                           flat(cmp_pos_k[l]), flat(cmp_pos_v[l]),
                           cmp_w1_k[l].astype(BF16), cmp_w2_k[l].astype(BF16),
                           cmp_w1_v[l].astype(BF16), cmp_w2_v[l].astype(BF16),
                           nsa_k_norm_cmp[l][None])
        o_ns = _nsa(p, p_small, kc, vc, nsa_q_norm[l][None], nsa_k_norm_slc[l][None], nsa_k_norm_win[l][None],
                    ovt, batch, seq)
        x2 = _mix(x2, o_dn, o_ns, p, w_proj_dn[l].astype(BF16), w_proj_nsa[l].astype(BF16),
                  w_out[l].astype(BF16))
        x2 = _ffn(x2, norm_mlp[l][None], w_up[l].astype(BF16), w_down[l].astype(BF16))
    return x2.reshape(batch, seq, D_MODEL)
```

```python
import functools
import math

import numpy as np
import jax
import jax.numpy as jnp
from jax import lax
from jax.experimental import pallas as pl
from jax.experimental.pallas import tpu as pltpu

F32 = jnp.float32
BF16 = jnp.bfloat16

D_MODEL = 1024
DN_HEADS = 8
DN_HEAD_DIM = 128
DN_W = DN_HEADS * DN_HEAD_DIM
DN_CONV = 4
NSA_HEADS = 16
NSA_GROUPS = 4
NSA_REP = NSA_HEADS // NSA_GROUPS
NSA_HEAD_DIM = 64
NSA_W = NSA_HEADS * NSA_HEAD_DIM
NSA_KV_W = NSA_GROUPS * NSA_HEAD_DIM
CMP_LEN = 32
CMP_STRIDE = 16
CMP_HIDDEN = 2 * NSA_HEAD_DIM
SLC_LEN = 64
SLC_TOPK = 8
WINDOW = 256
D_FF = 4 * D_MODEL
EPS = 1e-6
NEG = -1e30
FORCE = 1e9
LANES = 128

DN_CHUNK = LANES
DN_INTRA_GROUP = 8
ATT_TILE = WINDOW
ATT_ROW_BLOCK = 128

COL_Q, COL_K, COL_V, COL_Z = 0, 8, 16, 24
COL_NSQ, COL_GDN, COL_GNS = 32, 40, 48
COL_KC, COL_VC, COL_KS, COL_VS, COL_KW, COL_VW = 56, 58, 60, 62, 64, 66
COL_SMALL = 68
P_WIDTH = 72 * LANES
SMALL_B, SMALL_A, SMALL_GATE = 0, DN_HEADS, 2 * DN_HEADS

X_SLOPE = SLC_TOPK * 4

VMEM_LIMIT = 56 * 1024 * 1024


def _cparams(n_axes):
    return pltpu.CompilerParams(dimension_semantics=("arbitrary",) * n_axes,
                                vmem_limit_bytes=VMEM_LIMIT)


def _bdot(a, b):
    return jnp.dot(a.astype(BF16), b.astype(BF16), preferred_element_type=F32)


def _dot_nt(a, b):
    return lax.dot_general(a.astype(BF16), b.astype(BF16), (((1,), (1,)), ((), ())),
                           preferred_element_type=F32)


def _dot_tn(a, b):
    return lax.dot_general(a.astype(BF16), b.astype(BF16), (((0,), (0,)), ((), ())),
                           preferred_element_type=F32)


def _split3(a):
    a1 = a.astype(BF16)
    r = a - a1.astype(F32)
    a2 = r.astype(BF16)
    a3 = (r - a2.astype(F32)).astype(BF16)
    return a1, a2, a3


def _hi_lo(a):
    hi = a.astype(BF16)
    return hi, (a - hi.astype(F32)).astype(BF16)


def _dot_hl(x, y):
    (xh, xl), (yh, yl) = x, y
    return (jnp.dot(jnp.concatenate([xh, xl], axis=1), jnp.concatenate([yh, yh], axis=0),
                    preferred_element_type=F32)
            + jnp.dot(xh, yl, preferred_element_type=F32))


def _dot_sel_rhs(a, sel):
    return sum(jnp.dot(t, sel, preferred_element_type=F32) for t in _split3(a))


def _dot_sel_lhs(sel, b):
    return sum(jnp.dot(sel, t, preferred_element_type=F32) for t in _split3(b))


def _rms(x, gain):
    return x * lax.rsqrt(jnp.mean(x * x, axis=-1, keepdims=True) + EPS) * gain


def _inproj_kernel(x_ref, g_ref, w_ref, ws_ref, o_ref, os_ref, h_ref):
    @pl.when(pl.program_id(1) == 0)
    def _():
        h_ref[...] = _rms(x_ref[...], g_ref[...]).astype(BF16)
        os_ref[...] = jnp.dot(h_ref[...], ws_ref[...], preferred_element_type=F32)

    o_ref[...] = jnp.dot(h_ref[...], w_ref[...], preferred_element_type=F32).astype(BF16)


def _in_proj(x2, gain, w_perm, tm=1024, tn=1536):
    t = x2.shape[0]
    return pl.pallas_call(
        _inproj_kernel,
        grid=(t // tm, P_WIDTH // tn),
        in_specs=[pl.BlockSpec((tm, D_MODEL), lambda i, j: (i, 0)),
                  pl.BlockSpec((1, D_MODEL), lambda i, j: (0, 0)),
                  pl.BlockSpec((D_MODEL, tn), lambda i, j: (0, j)),
                  pl.BlockSpec((D_MODEL, LANES), lambda i, j: (0, COL_SMALL))],
        out_specs=[pl.BlockSpec((tm, tn), lambda i, j: (i, j)),
                   pl.BlockSpec((tm, LANES), lambda i, j: (i, 0))],
        out_shape=[jax.ShapeDtypeStruct((t, P_WIDTH), BF16), jax.ShapeDtypeStruct((t, LANES), F32)],
        scratch_shapes=[pltpu.VMEM((tm, D_MODEL), BF16)],
        compiler_params=_cparams(2),
        name="in_proj",
    )(x2, gain, w_perm, w_perm)


def _conv_silu(x, w):
    rows = lax.broadcasted_iota(jnp.int32, x.shape, 0)
    y = x * w[DN_CONV - 1:DN_CONV, :]
    for j in range(DN_CONV - 1):
        sh = DN_CONV - 1 - j
        xs = jnp.where(rows >= sh, pltpu.roll(x, sh, axis=0), 0.0)
        y = y + xs * w[j:j + 1, :]
    return y * jax.nn.sigmoid(y)


def _gdn_kernel(alog_ref, dtb_ref, q_ref, k_ref, v_ref, z_ref, sm_ref, cq_ref, ck_ref, cv_ref,
                onorm_ref, o_ref, qs, ks, vs, gs, bs, us, ws, qks, os_, gls):
    h = pl.program_id(1)
    s_len = q_ref.shape[0]
    c = DN_CHUNK
    n_chunks = s_len // c

    q = _conv_silu(q_ref[...].astype(F32), cq_ref[...])
    k = _conv_silu(k_ref[...].astype(F32), ck_ref[...])
    v = _conv_silu(v_ref[...].astype(F32), cv_ref[...])
    qs[...] = q * lax.rsqrt(jnp.sum(q * q, axis=-1, keepdims=True) + EPS) * (DN_HEAD_DIM ** -0.5)
    ks[...] = k * lax.rsqrt(jnp.sum(k * k, axis=-1, keepdims=True) + EPS)
    vs[...] = v

    kk = lax.broadcasted_iota(jnp.int32, (LANES, 2 * LANES), 0)
    nn = lax.broadcasted_iota(jnp.int32, (LANES, 2 * LANES), 1)
    onehot = (kk == jnp.where(nn < LANES, SMALL_B + h, SMALL_A + h)).astype(BF16)
    ba = _dot_sel_rhs(sm_ref[...], onehot)
    neg_a = -jnp.exp(jnp.full((1, LANES), alog_ref[0, h], F32))
    dtb = jnp.full((1, LANES), dtb_ref[0, h], F32)
    bs[...] = jax.nn.sigmoid(ba[:, :LANES])
    gs[...] = neg_a * jax.nn.softplus(ba[:, LANES:] + dtb)

    ri = lax.broadcasted_iota(jnp.int32, (c, c), 0)
    ci = lax.broadcasted_iota(jnp.int32, (c, c), 1)
    causal = ri >= ci
    strict = ri > ci
    eye = (ri == ci).astype(F32)
    ltri = causal.astype(BF16)

    def intra(ig, carry):
        grp = range(DN_INTRA_GROUP)
        rs = [pl.ds((ig * DN_INTRA_GROUP + j) * c, c) for j in grp]
        qc, kc, vc, bc = ([ref[r, :] for r in rs] for ref in (qs, ks, vs, bs))
        gcum = [_dot_sel_lhs(ltri, gs[r, :]) for r in rs]
        decay = [jnp.exp(jnp.where(causal, g - g.T, -jnp.inf)) for g in gcum]
        kb = [kc[j] * bc[j] for j in grp]
        a = [jnp.where(strict, _dot_nt(kb[j], kc[j]) * decay[j], 0.0) for j in grp]
        tinv = [eye - a[j] for j in grp]
        pw = [_hi_lo(x) for x in a]
        for _ in range(int(math.log2(c)) - 1):
            pw = [_hi_lo(_dot_hl(x, x)) for x in pw]
            tinv = [tinv[j] + _dot_hl(_hi_lo(tinv[j]), pw[j]) for j in grp]
        eg = [jnp.exp(g) for g in gcum]
        uw = [_bdot(tinv[j], jnp.concatenate([vc[j] * bc[j], kb[j] * eg[j]], axis=1)) for j in grp]
        qk = [_dot_nt(qc[j], kc[j]) * decay[j] for j in grp]
        glast = [g[c - 1:c, :] for g in gcum]
        k_end = [kc[j] * jnp.exp(glast[j] - gcum[j]) for j in grp]
        qk_uw = [_bdot(qk[j], uw[j]) for j in grp]
        ke_uw = [_dot_tn(k_end[j], uw[j]) for j in grp]
        for j in grp:
            r = rs[j]
            qs[r, :] = qc[j] * eg[j] - qk_uw[j][:, LANES:]
            qks[r, :] = qk_uw[j][:, :LANES]
            ws[r, :] = ke_uw[j][:, LANES:]
            us[r, :] = ke_uw[j][:, :LANES]
            gls[pl.ds(ig * DN_INTRA_GROUP + j, 1), :] = jnp.exp(glast[j])
        return carry

    for ig in range(n_chunks // DN_INTRA_GROUP):
        intra(ig, 0)

    state = jnp.zeros((DN_HEAD_DIM, DN_HEAD_DIM), F32)
    for ic in range(n_chunks):
        r = pl.ds(ic * c, c)
        os_[r, :] = _bdot(qs[r, :], state) + qks[r, :]
        state = state * gls[pl.ds(ic, 1), :] - _bdot(ws[r, :], state) + us[r, :]

    z = z_ref[...].astype(F32)
    o_ref[...] = (_rms(os_[...], onorm_ref[...]) * (z * jax.nn.sigmoid(z))).astype(BF16)


def _gdn(p, p_small, conv_w, a_log, dt_bias, out_norm, batch, seq):
    row_blk = lambda col0: pl.BlockSpec((seq, LANES), lambda b, h, col0=col0: (b, col0 + h))
    conv_blk = lambda col0: pl.BlockSpec((DN_CONV, LANES), lambda b, h, col0=col0: (0, col0 + h))
    smem = pl.BlockSpec(memory_space=pltpu.SMEM)
    return pl.pallas_call(
        _gdn_kernel,
        grid=(batch, DN_HEADS),
        in_specs=[smem, smem,
                  row_blk(COL_Q), row_blk(COL_K), row_blk(COL_V), row_blk(COL_Z),
                  pl.BlockSpec((seq, LANES), lambda b, h: (b, 0)),
                  conv_blk(COL_Q), conv_blk(COL_K), conv_blk(COL_V),
                  pl.BlockSpec((1, LANES), lambda b, h: (0, 0))],
        out_specs=pl.BlockSpec((seq, LANES), lambda b, h: (b, h)),
        out_shape=jax.ShapeDtypeStruct((batch * seq, DN_W), BF16),
        scratch_shapes=([pltpu.VMEM((seq, LANES), F32) for _ in range(9)]
                        + [pltpu.VMEM((seq // DN_CHUNK, LANES), F32)]),
        compiler_params=_cparams(2),
        name="gdn",
    )(a_log, dt_bias, p, p, p, p, p_small, conv_w, conv_w, conv_w, out_norm)


def _gelu_tanh(x):
    return 0.5 * x * (1.0 + jnp.tanh(math.sqrt(2.0 / math.pi) * (x + 0.044715 * (x * x * x))))


def _compress_one(t2, pos, w1, w2):
    n = t2.shape[0]
    half = t2.shape[1]
    y1 = _bdot(t2 + pos[:, :half], w1[:half, :])
    y2 = _bdot(t2 + pos[:, half:], w1[half:, :])
    hid = _gelu_tanh(y1 + pltpu.roll(y2, n - 1, axis=0))
    return _bdot(hid, w2)


def _compress_kernel(tk_ref, tv_ref, pk_ref, pv_ref, w1k_ref, w2k_ref, w1v_ref, w2v_ref, kn_ref,
                     kc_ref, vc_ref):
    kc = _compress_one(tk_ref[0, 0].astype(F32), pk_ref[...], w1k_ref[...], w2k_ref[...])
    kc_ref[0, 0] = _rms(kc, kn_ref[...])
    vc_ref[0, 0] = _compress_one(tv_ref[0, 0].astype(F32), pv_ref[...], w1v_ref[...], w2v_ref[...])


def _compress(t2k, t2v, pos_k, pos_v, w1k, w2k, w1v, w2v, kn_cmp):
    batch, groups, n, width = t2k.shape
    d = NSA_HEAD_DIM
    tok = pl.BlockSpec((1, 1, n, width), lambda b, g: (b, g, 0, 0))
    full = lambda a: pl.BlockSpec(a.shape, lambda b, g: (0,) * a.ndim)
    out = pl.BlockSpec((1, 1, n, d), lambda b, g: (b, g, 0, 0))
    return pl.pallas_call(
        _compress_kernel,
        grid=(batch, groups),
        in_specs=[tok, tok, full(pos_k), full(pos_v), full(w1k), full(w2k), full(w1v), full(w2v),
                  full(kn_cmp)],
        out_specs=[out, out],
        out_shape=[jax.ShapeDtypeStruct((batch, groups, n, d), F32)] * 2,
        compiler_params=_cparams(2),
        name="nsa_compress",
    )(t2k, t2v, pos_k, pos_v, w1k, w2k, w1v, w2v, kn_cmp)


def _nsa_kernel(q_ref, ks_ref, vs_ref, kw_ref, vw_ref, kc_ref, vc_ref, sm_ref, qn_ref, kns_ref,
                knw_ref, ovt_ref, gsel_ref, hsum_ref, o_ref, ksa, vsa, kwa, vwa, kca, vca, m_slc, acc_slc,
                m_win, acc_win, tile_used):
    it = pl.program_id(1)
    tile = ATT_TILE
    s_len = ks_ref.shape[0]
    d = NSA_HEAD_DIM
    rep = NSA_REP
    rows = rep * tile
    n_sel = s_len // SLC_LEN
    n_cend = kc_ref.shape[2]

    @pl.when(it == 0)
    def _():
        rowi = lax.broadcasted_iota(jnp.int32, (s_len, d), 0)
        xl = lax.broadcasted_iota(jnp.int32, (s_len, d), 1)
        in_tile = (rowi & (tile - 1)).astype(F32)
        tile_start = (rowi - (rowi & (tile - 1))).astype(F32)
        pos_cols = jnp.where((xl == X_SLOPE) | (xl == X_SLOPE + 1), in_tile,
                             jnp.where((xl == X_SLOPE + 2) | (xl == X_SLOPE + 3), tile_start, 0.0))
        blk_cols = jnp.where(xl == lax.shift_right_logical(rowi, int(math.log2(SLC_LEN))), 1.0, 0.0)
        ci = lax.broadcasted_iota(jnp.int32, (n_cend, d), 0)
        cl = lax.broadcasted_iota(jnp.int32, (n_cend, d), 1)
        cend_cols = jnp.where((cl == X_SLOPE) | (cl == X_SLOPE + 1), (CMP_STRIDE * ci).astype(F32), 0.0)
        ones_s = jnp.ones((s_len, LANES), F32)
        ones_c = jnp.ones((n_cend, LANES), F32)
        for g in range(NSA_GROUPS):
            sl = slice(g * d, (g + 1) * d)
            ksa[g] = jnp.concatenate([_rms(ks_ref[:, sl].astype(F32), kns_ref[...]), pos_cols + blk_cols],
                                     axis=1).astype(BF16)
            kwa[g] = jnp.concatenate([_rms(kw_ref[:, sl].astype(F32), knw_ref[...]), pos_cols],
                                     axis=1).astype(BF16)
            kca[g] = jnp.concatenate([kc_ref[0, g], cend_cols], axis=1).astype(BF16)
            vs_g, vw_g = vs_ref[:, sl].astype(F32), vw_ref[:, sl].astype(F32)
            vsa[g] = jnp.concatenate([vs_g, vs_g, ones_s], axis=1).astype(BF16)
            vwa[g] = jnp.concatenate([vw_g, vw_g, ones_s], axis=1).astype(BF16)
            vca[g] = jnp.concatenate([vc_ref[0, g], vc_ref[0, g], ones_c], axis=1).astype(BF16)

    t0 = it * tile
    groups = range(NSA_GROUPS)
    row_r =lax.shift_right_logical(lax.broadcasted_iota(jnp.int32, (rows, d), 0),
                                    int(math.log2(tile)))
    tpos = t0 + (lax.broadcasted_iota(jnp.int32, (rows, LANES), 0) & (tile - 1))
    xl = lax.broadcasted_iota(jnp.int32, (rows, d), 1)
    col = lax.broadcasted_iota(jnp.int32, (rows, tile), 1)
    row_tt = lax.broadcasted_iota(jnp.int32, (rows, tile), 0) & (tile - 1)
    lower = col <= row_tt
    upper = col > row_tt
    cend = CMP_STRIDE * lax.broadcasted_iota(jnp.int32, (1, LANES), 1) + (CMP_LEN - 1)
    cmp_ok = tpos >= cend
    any_valid = (tpos >= CMP_LEN - 1).astype(F32)

    def attend(items):
        def scores(x, rb):
            r = slice(rb * ATT_ROW_BLOCK, (rb + 1) * ATT_ROW_BLOCK)
            s = _dot_nt(x[0][r], x[1])
            if x[3] is not None:
                s = jnp.where(x[3][r], s, NEG)
            return s, jnp.max(s, axis=-1, keepdims=True)

        def update(x, rb, s, s_max):
            r = slice(rb * ATT_ROW_BLOCK, (rb + 1) * ATT_ROW_BLOCK)
            g = x[6]
            m_old = x[4][g, r, :]
            m_new = jnp.maximum(m_old, jnp.broadcast_to(s_max, m_old.shape))
            alpha = jnp.exp2(m_old - m_new)
            p = jnp.exp2(s - jnp.concatenate([m_new] * (tile // LANES), axis=1))
            x[5][g, r, :] = jnp.concatenate([alpha, alpha], axis=1) * x[5][g, r, :] + _bdot(p, x[2])
            x[4][g, r, :] = m_new

        pending = None
        for x in items:
            for rb in range(rows // ATT_ROW_BLOCK):
                cur = (x, rb) + scores(x, rb)
                if pending is not None:
                    update(*pending)
                pending = cur
        update(*pending)

    log2e = math.log2(math.e)
    qf = q_ref[...].astype(F32)
    hw = hsum_ref.shape[0]
    sq_hi, sq_lo = _hi_lo(qf * qf)
    ssq = jnp.concatenate(
        [jnp.dot(sq_hi[:, j:j + hw], hsum_ref[...], preferred_element_type=F32)
         + jnp.dot(sq_lo[:, j:j + hw], hsum_ref[...], preferred_element_type=F32)
         for j in range(0, NSA_W, hw)], axis=1)
    qn = qf * lax.rsqrt(ssq * (1.0 / d) + EPS) * (qn_ref[...] * (log2e * d ** -0.5))
    q64, q_x, q_aug = [], [], []
    for g in groups:
        slopes = [log2e * 2.0 ** (-8.0 * (g * rep + r + 1) / NSA_HEADS) for r in range(rep)]
        sl = jnp.full((rows, d), slopes[rep - 1], F32)
        for r in range(rep - 2, -1, -1):
            sl = jnp.where(row_r == r, slopes[r], sl)
        sl_hi = sl.astype(BF16).astype(F32)
        q_x.append(jnp.where((xl == X_SLOPE) | (xl == X_SLOPE + 2), sl_hi,
                             jnp.where((xl == X_SLOPE + 1) | (xl == X_SLOPE + 3), sl - sl_hi, 0.0)))
        q64.append(jnp.concatenate(
            [qn[:, (g * rep + r) * d:(g * rep + r + 1) * d] for r in range(rep)], axis=0))
        q_aug.append(jnp.concatenate([q64[g], q_x[g]], axis=1).astype(BF16))

    s_c = [jnp.where(cmp_ok, _dot_nt(q_aug[g], kca[g]), NEG) for g in groups]
    e_c = [jnp.exp2(s - jnp.max(s, axis=-1, keepdims=True)) for s in s_c]
    pv_c = [_bdot(e_c[g], vca[g]) for g in groups]
    w_c = [any_valid / pv[:, LANES:] for pv in pv_c]
    p_c = [e_c[g] * w_c[g] for g in groups]
    o_cmp = [pv_c[g][:, :LANES] * w_c[g] for g in groups]
    p_sum = [sum(p[r * tile:(r + 1) * tile] for r in range(rep)) for p in p_c]

    jj = lax.broadcasted_iota(jnp.int32, (X_SLOPE, tile), 0)
    blk_t = lax.shift_right_logical(t0 + lax.broadcasted_iota(jnp.int32, (1, tile), 1), int(math.log2(SLC_LEN)))
    imp = [sum(_dot_nt(ovt_ref[...], t) for t in _split3(ps))[:X_SLOPE] for ps in p_sum]
    imp = [jnp.where(jj <= blk_t, x, NEG) for x in imp]
    imp = [jnp.where((jj == blk_t) | (jj == 0), FORCE, x) for x in imp]
    imp = [jnp.where(jj < n_sel, x, -3e38) for x in imp]
    rank = [jnp.zeros((X_SLOPE, tile), F32) for _ in groups]
    for j in range(n_sel):
        for g in groups:
            row = imp[g][j:j + 1, :]
            ge = jnp.where(row >= imp[g], 1.0, 0.0)
            gt = jnp.where(row > imp[g], 1.0, 0.0)
            rank[g] = rank[g] + jnp.where(jj > j, ge, gt)
    k_sel = float(min(SLC_TOPK, n_sel))
    bias_t = [jnp.where((rk < k_sel) | (jj >= n_sel), 0.0, NEG) for rk in rank]
    zero_rows = jnp.zeros((LANES - X_SLOPE, tile), F32)
    sel_bias = [jnp.concatenate([bt, zero_rows], axis=0).T[:, :d] for bt in bias_t]
    q_sel = [jnp.concatenate([q64[g], q_x[g] + jnp.concatenate([sel_bias[g]] * rep, axis=0)],
                             axis=1).astype(BF16) for g in groups]

    for g in groups:
        m_slc[g] = jnp.full((rows, LANES), -jnp.inf, F32)
        m_win[g] = jnp.full((rows, LANES), -jnp.inf, F32)
        acc_slc[g] = jnp.zeros((rows, 2 * LANES), F32)
        acc_win[g] = jnp.zeros((rows, 2 * LANES), F32)

    r_diag = pl.ds(pl.multiple_of(t0, tile), tile)
    attend([(q_sel[g], ksa[g, r_diag, :], vsa[g, r_diag, :], lower, m_slc, acc_slc, g) for g in groups])
    attend([(q_aug[g], kwa[g, r_diag, :], vwa[g, r_diag, :], lower, m_win, acc_win, g) for g in groups])

    @pl.when(it > 0)
    def _():
        r_prev = pl.ds(pl.multiple_of(t0 - tile, tile), tile)
        attend([(q_aug[g], kwa[g, r_prev, :], vwa[g, r_prev, :], upper, m_win, acc_win, g) for g in groups])

    picked = functools.reduce(jnp.maximum, [jnp.where(rk < k_sel, 1.0, 0.0) for rk in rank])
    blocks_per_tile = tile // SLC_LEN
    for kt in range(s_len // tile):
        tile_used[kt] = jnp.max(picked[kt * blocks_per_tile:(kt + 1) * blocks_per_tile, :]).astype(jnp.int32)

    def slc_tile(kt, carry):
        @pl.when(tile_used[kt] > 0)
        def _():
            r = pl.ds(pl.multiple_of(kt * tile, tile), tile)
            attend([(q_sel[g], ksa[g, r, :], vsa[g, r, :], None, m_slc, acc_slc, g) for g in groups])

        return carry

    lax.fori_loop(0, it, slc_tile, 0)

    g_hi, g_lo = _hi_lo(jax.nn.sigmoid(sm_ref[...]))
    gate_w = (jnp.dot(g_hi, gsel_ref[...], preferred_element_type=F32)
              + jnp.dot(g_lo, gsel_ref[...], preferred_element_type=F32))
    low_half = lax.broadcasted_iota(jnp.int32, (tile, LANES), 1) < d
    for g in groups:
        o_slc = acc_slc[g][:, :LANES] / acc_slc[g][:, LANES:]
        o_win = acc_win[g][:, :LANES] / acc_win[g][:, LANES:]
        for pr in range(rep // 2):
            pair = g * (rep // 2) + pr
            r_even = slice(2 * pr * tile, (2 * pr + 1) * tile)
            r_odd = slice((2 * pr + 1) * tile, (2 * pr + 2) * tile)
            merged = 0.0
            for br, o in enumerate((o_cmp[g], o_slc, o_win)):
                c0 = (pair * 3 + br) * LANES
                merged = merged + gate_w[:, c0:c0 + LANES] * jnp.where(low_half, o[r_even], o[r_odd])
            o_ref[:, pair * LANES:(pair + 1) * LANES] = merged.astype(BF16)


def _gate_spread():
    n_pairs = NSA_HEADS // 2
    c = np.arange(LANES)[:, None]
    col = np.arange(n_pairs * 3 * LANES)[None, :]
    pair, br, n = col // (3 * LANES), (col // LANES) % 3, col % LANES
    head = 2 * pair + (n >= NSA_HEAD_DIM)
    return jnp.asarray((c == SMALL_GATE + 3 * head + br).astype(np.float32), BF16)


def _head_sum(width=2 * LANES):
    i = np.arange(width)
    return jnp.asarray((i[:, None] // NSA_HEAD_DIM == i[None, :] // NSA_HEAD_DIM).astype(np.float32), BF16)


def _nsa(p, p_small, kc, vc, q_norm, kn_slc, kn_win, ovt, batch, seq):
    gsel, hsum = _gate_spread(), _head_sum()
    q_gain = jnp.tile(q_norm, (1, NSA_HEADS))
    tile = ATT_TILE
    nt = seq // tile
    rows = NSA_REP * tile
    kv = lambda col: pl.BlockSpec((seq, NSA_KV_W), lambda b, i, col=col: (b, col // 2),
                                  pipeline_mode=pl.Buffered(1))
    full = lambda a: pl.BlockSpec(a.shape, lambda b, i: (0,) * a.ndim)
    cmp_blk = pl.BlockSpec((1,) + kc.shape[1:], lambda b, i: (b, 0, 0, 0))
    aug = lambda n: pltpu.VMEM((NSA_GROUPS, n, LANES), BF16)
    val = lambda n: pltpu.VMEM((NSA_GROUPS, n, 2 * LANES), BF16)
    stat = pltpu.VMEM((NSA_GROUPS, rows, LANES), F32)
    acc = pltpu.VMEM((NSA_GROUPS, rows, 2 * LANES), F32)
    return pl.pallas_call(
        _nsa_kernel,
        grid=(batch, nt),
        in_specs=[pl.BlockSpec((tile, NSA_W), lambda b, i: (b * nt + i, COL_NSQ // 8)),
                  kv(COL_KS), kv(COL_VS), kv(COL_KW), kv(COL_VW),
                  cmp_blk, cmp_blk,
                  pl.BlockSpec((tile, LANES), lambda b, i: (b * nt + i, 0)),
                  full(q_gain), full(kn_slc), full(kn_win), full(ovt), full(gsel), full(hsum)],
        out_specs=pl.BlockSpec((tile, NSA_W), lambda b, i: (b * nt + i, 0)),
        out_shape=jax.ShapeDtypeStruct((batch * seq, NSA_W), BF16),
        scratch_shapes=[aug(seq), val(seq), aug(seq), val(seq), aug(kc.shape[2]), val(kc.shape[2]),
                        stat, acc, stat, acc, pltpu.SMEM((nt,), jnp.int32)],
        compiler_params=_cparams(2),
        name="nsa_attention",
    )(p, p, p, p, p, kc, vc, p_small, q_gain, kn_slc, kn_win, ovt, gsel, hsum)


def _mix_kernel(x_ref, odn_ref, ons_ref, gdn_ref, gns_ref, wdn_ref, wns_ref, wout_ref, o_ref):
    y_dn = jnp.dot(odn_ref[...], wdn_ref[...], preferred_element_type=F32)
    y_ns = jnp.dot(ons_ref[...], wns_ref[...], preferred_element_type=F32)
    mix = (jax.nn.sigmoid(gdn_ref[...].astype(F32)) * y_dn
           + jax.nn.sigmoid(gns_ref[...].astype(F32)) * y_ns)
    o_ref[...] = x_ref[...] + jnp.dot(mix.astype(BF16), wout_ref[...], preferred_element_type=F32)


def _mix(x2, o_dn, o_ns, p, w_dn, w_ns, w_out, tm=512):
    t = x2.shape[0]
    row = lambda col: pl.BlockSpec((tm, D_MODEL), lambda i, col=col: (i, col))
    wfull = pl.BlockSpec((D_MODEL, D_MODEL), lambda i: (0, 0))
    return pl.pallas_call(
        _mix_kernel,
        grid=(t // tm,),
        in_specs=[row(0), row(0), row(0), row(COL_GDN // 8), row(COL_GNS // 8), wfull, wfull, wfull],
        out_specs=row(0),
        out_shape=jax.ShapeDtypeStruct((t, D_MODEL), F32),
        compiler_params=_cparams(1),
        name="mix_out",
    )(x2, o_dn, o_ns, p, p, w_dn, w_ns, w_out)


def _ffn_kernel(x_ref, g_ref, wup_ref, wdown_ref, o_ref, h_ref, acc_ref):
    f = pl.program_id(1)

    @pl.when(f == 0)
    def _():
        h_ref[...] = _rms(x_ref[...], g_ref[...]).astype(BF16)
        acc_ref[...] = jnp.zeros_like(acc_ref)

    u = jnp.dot(h_ref[...], wup_ref[...], preferred_element_type=F32)
    u = jnp.square(jnp.maximum(u, 0.0)).astype(BF16)
    acc_ref[...] += jnp.dot(u, wdown_ref[...], preferred_element_type=F32)

    @pl.when(f == pl.num_programs(1) - 1)
    def _():
        o_ref[...] = x_ref[...] + acc_ref[...]


def _ffn(x2, gain, w_up, w_down, tm=1024, tf=1024):
    t = x2.shape[0]
    return pl.pallas_call(
        _ffn_kernel,
        grid=(t // tm, D_FF // tf),
        in_specs=[pl.BlockSpec((tm, D_MODEL), lambda i, f: (i, 0)),
                  pl.BlockSpec((1, D_MODEL), lambda i, f: (0, 0)),
                  pl.BlockSpec((D_MODEL, tf), lambda i, f: (0, f)),
                  pl.BlockSpec((tf, D_MODEL), lambda i, f: (f, 0))],
        out_specs=pl.BlockSpec((tm, D_MODEL), lambda i, f: (i, 0)),
        out_shape=jax.ShapeDtypeStruct((t, D_MODEL), F32),
        scratch_shapes=[pltpu.VMEM((tm, D_MODEL), BF16), pltpu.VMEM((tm, D_MODEL), F32)],
        compiler_params=_cparams(2),
        name="ffn",
    )(x2, gain, w_up, w_down)


def _overlap_matrix_t(seq):
    n_cmp = (seq - CMP_LEN) // CMP_STRIDE + 1
    n_sel = seq // SLC_LEN
    c0 = np.arange(n_cmp)[None, :] * CMP_STRIDE
    j0 = np.arange(n_sel)[:, None] * SLC_LEN
    ov = np.clip(np.minimum(c0 + CMP_LEN, j0 + SLC_LEN) - np.maximum(c0, j0), 0, None) / CMP_LEN
    out = np.zeros((LANES, LANES), np.float32)
    out[:n_sel, :n_cmp] = ov
    return jnp.asarray(out, BF16)


def _regroup_w_in(w):
    o_b = 4 * DN_W
    o_q = o_b + 2 * DN_HEADS
    o_kv = o_q + NSA_W
    o_gate = o_kv + 6 * NSA_KV_W
    o_gdn = o_gate + 3 * NSA_HEADS
    pad = jnp.zeros((w.shape[0], P_WIDTH - w.shape[1]), w.dtype)
    return jnp.concatenate(
        [w[:, :o_b], w[:, o_q:o_kv], w[:, o_gdn:], w[:, o_kv:o_gate], w[:, o_b:o_q],
         w[:, o_gate:o_gdn], pad], axis=1).astype(BF16)


def _blocks16(p, col, batch, seq):
    t = p[:, col * LANES:col * LANES + NSA_KV_W]
    t = t.reshape(batch, seq // CMP_STRIDE, CMP_STRIDE, NSA_GROUPS, NSA_HEAD_DIM)
    t = jnp.transpose(t, (0, 3, 1, 2, 4))
    return t.reshape(batch, NSA_GROUPS, seq // CMP_STRIDE, CMP_STRIDE * NSA_HEAD_DIM)


def kernel(x, norm_mix, w_in, dn_conv, dn_a_log, dn_dt_bias, dn_out_norm, nsa_q_norm, nsa_k_norm_cmp,
           nsa_k_norm_slc, nsa_k_norm_win, cmp_pos_k, cmp_w1_k, cmp_w2_k, cmp_pos_v, cmp_w1_v, cmp_w2_v,
           w_proj_dn, w_proj_nsa, w_out, norm_mlp, w_up, w_down):
    batch, seq, _ = x.shape
    assert seq // CMP_STRIDE == LANES and seq % ATT_TILE == 0 and seq // SLC_LEN <= X_SLOPE
    ovt = _overlap_matrix_t(seq)
    x2 = x.reshape(batch * seq, D_MODEL)
    for l in range(w_in.shape[0]):
        p, p_small = _in_proj(x2, norm_mix[l][None], _regroup_w_in(w_in[l]))
        o_dn = _gdn(p, p_small, dn_conv[l], dn_a_log[l][None], dn_dt_bias[l][None], dn_out_norm[l][None],
                    batch, seq)
        flat = lambda a: a.reshape(1, -1)
        kc, vc = _compress(_blocks16(p, COL_KC, batch, seq), _blocks16(p, COL_VC, batch, seq),
                           flat(cmp_pos_k[l]), flat(cmp_pos_v[l]),
                           cmp_w1_k[l].astype(BF16), cmp_w2_k[l].astype(BF16),
                           cmp_w1_v[l].astype(BF16), cmp_w2_v[l].astype(BF16),
                           nsa_k_norm_cmp[l][None])
        o_ns = _nsa(p, p_small, kc, vc, nsa_q_norm[l][None], nsa_k_norm_slc[l][None], nsa_k_norm_win[l][None],
                    ovt, batch, seq)
        x2 = _mix(x2, o_dn, o_ns, p, w_proj_dn[l].astype(BF16), w_proj_nsa[l].astype(BF16),
                  w_out[l].astype(BF16))
        x2 = _ffn(x2, norm_mlp[l][None], w_up[l].astype(BF16), w_down[l].astype(BF16))
    return x2.reshape(batch, seq, D_MODEL)
```

```python
import functools
import math

import numpy as np
import jax
import jax.numpy as jnp
from jax import lax
from jax.experimental import pallas as pl
from jax.experimental.pallas import tpu as pltpu

F32 = jnp.float32
BF16 = jnp.bfloat16

D_MODEL = 1024
DN_HEADS = 8
DN_HEAD_DIM = 128
DN_W = DN_HEADS * DN_HEAD_DIM
DN_CONV = 4
NSA_HEADS = 16
NSA_GROUPS = 4
NSA_REP = NSA_HEADS // NSA_GROUPS
NSA_HEAD_DIM = 64
NSA_W = NSA_HEADS * NSA_HEAD_DIM
NSA_KV_W = NSA_GROUPS * NSA_HEAD_DIM
CMP_LEN = 32
CMP_STRIDE = 16
CMP_HIDDEN = 2 * NSA_HEAD_DIM
SLC_LEN = 64
SLC_TOPK = 8
WINDOW = 256
D_FF = 4 * D_MODEL
EPS = 1e-6
NEG = -1e30
FORCE = 1e9
LANES = 128

DN_CHUNK = LANES
DN_INTRA_GROUP = 8
ATT_TILE = WINDOW
ATT_ROW_BLOCK = 128

COL_Q, COL_K, COL_V, COL_Z = 0, 8, 16, 24
COL_NSQ, COL_GDN, COL_GNS = 32, 40, 48
COL_KC, COL_VC, COL_KS, COL_VS, COL_KW, COL_VW = 56, 58, 60, 62, 64, 66
COL_SMALL = 68
P_WIDTH = 72 * LANES
SMALL_B, SMALL_A, SMALL_GATE = 0, DN_HEADS, 2 * DN_HEADS

X_SLOPE = SLC_TOPK * 4

VMEM_LIMIT = 56 * 1024 * 1024


def _cparams(n_axes):
    return pltpu.CompilerParams(dimension_semantics=("arbitrary",) * n_axes,
                                vmem_limit_bytes=VMEM_LIMIT)


def _bdot(a, b):
    return jnp.dot(a.astype(BF16), b.astype(BF16), preferred_element_type=F32)


def _dot_nt(a, b):
    return lax.dot_general(a.astype(BF16), b.astype(BF16), (((1,), (1,)), ((), ())),
                           preferred_element_type=F32)


def _dot_tn(a, b):
    return lax.dot_general(a.astype(BF16), b.astype(BF16), (((0,), (0,)), ((), ())),
                           preferred_element_type=F32)


def _split3(a):
    a1 = a.astype(BF16)
    r = a - a1.astype(F32)
    a2 = r.astype(BF16)
    a3 = (r - a2.astype(F32)).astype(BF16)
    return a1, a2, a3


def _hi_lo(a):
    hi = a.astype(BF16)
    return hi, (a - hi.astype(F32)).astype(BF16)


def _dot_hl(x, y):
    (xh, xl), (yh, yl) = x, y
    return (jnp.dot(jnp.concatenate([xh, xl], axis=1), jnp.concatenate([yh, yh], axis=0),
                    preferred_element_type=F32)
            + jnp.dot(xh, yl, preferred_element_type=F32))


def _dot_sel_rhs(a, sel):
    return sum(jnp.dot(t, sel, preferred_element_type=F32) for t in _split3(a))


def _dot_sel_lhs(sel, b):
    return sum(jnp.dot(sel, t, preferred_element_type=F32) for t in _split3(b))


def _rms(x, gain):
    return x * lax.rsqrt(jnp.mean(x * x, axis=-1, keepdims=True) + EPS) * gain


def _inproj_kernel(x_ref, g_ref, w_ref, ws_ref, o_ref, os_ref, h_ref):
    @pl.when(pl.program_id(1) == 0)
    def _():
        h_ref[...] = _rms(x_ref[...], g_ref[...]).astype(BF16)
        os_ref[...] = jnp.dot(h_ref[...], ws_ref[...], preferred_element_type=F32)

    o_ref[...] = jnp.dot(h_ref[...], w_ref[...], preferred_element_type=F32).astype(BF16)


def _in_proj(x2, gain, w_perm, tm=1024, tn=1536):
    t = x2.shape[0]
    return pl.pallas_call(
        _inproj_kernel,
        grid=(t // tm, P_WIDTH // tn),
        in_specs=[pl.BlockSpec((tm, D_MODEL), lambda i, j: (i, 0)),
                  pl.BlockSpec((1, D_MODEL), lambda i, j: (0, 0)),
                  pl.BlockSpec((D_MODEL, tn), lambda i, j: (0, j)),
                  pl.BlockSpec((D_MODEL, LANES), lambda i, j: (0, COL_SMALL))],
        out_specs=[pl.BlockSpec((tm, tn), lambda i, j: (i, j)),
                   pl.BlockSpec((tm, LANES), lambda i, j: (i, 0))],
        out_shape=[jax.ShapeDtypeStruct((t, P_WIDTH), BF16), jax.ShapeDtypeStruct((t, LANES), F32)],
        scratch_shapes=[pltpu.VMEM((tm, D_MODEL), BF16)],
        compiler_params=_cparams(2),
        name="in_proj",
    )(x2, gain, w_perm, w_perm)


def _conv_silu(x, w):
    rows = lax.broadcasted_iota(jnp.int32, x.shape, 0)
    y = x * w[DN_CONV - 1:DN_CONV, :]
    for j in range(DN_CONV - 1):
        sh = DN_CONV - 1 - j
        xs = jnp.where(rows >= sh, pltpu.roll(x, sh, axis=0), 0.0)
        y = y + xs * w[j:j + 1, :]
    return y * jax.nn.sigmoid(y)


def _gdn_kernel(alog_ref, dtb_ref, q_ref, k_ref, v_ref, z_ref, sm_ref, cq_ref, ck_ref, cv_ref,
                onorm_ref, o_ref, qs, ks, vs, gs, bs, us, ws, qks, os_, gls):
    h = pl.program_id(1)
    s_len = q_ref.shape[0]
    c = DN_CHUNK
    n_chunks = s_len // c

    q = _conv_silu(q_ref[...].astype(F32), cq_ref[...])
    k = _conv_silu(k_ref[...].astype(F32), ck_ref[...])
    v = _conv_silu(v_ref[...].astype(F32), cv_ref[...])
    qs[...] = q * lax.rsqrt(jnp.sum(q * q, axis=-1, keepdims=True) + EPS) * (DN_HEAD_DIM ** -0.5)
    ks[...] = k * lax.rsqrt(jnp.sum(k * k, axis=-1, keepdims=True) + EPS)
    vs[...] = v

    kk = lax.broadcasted_iota(jnp.int32, (LANES, 2 * LANES), 0)
    nn = lax.broadcasted_iota(jnp.int32, (LANES, 2 * LANES), 1)
    onehot = (kk == jnp.where(nn < LANES, SMALL_B + h, SMALL_A + h)).astype(BF16)
    ba = _dot_sel_rhs(sm_ref[...], onehot)
    neg_a = -jnp.exp(jnp.full((1, LANES), alog_ref[0, h], F32))
    dtb = jnp.full((1, LANES), dtb_ref[0, h], F32)
    bs[...] = jax.nn.sigmoid(ba[:, :LANES])
    gs[...] = neg_a * jax.nn.softplus(ba[:, LANES:] + dtb)

    ri = lax.broadcasted_iota(jnp.int32, (c, c), 0)
    ci = lax.broadcasted_iota(jnp.int32, (c, c), 1)
    causal = ri >= ci
    strict = ri > ci
    eye = (ri == ci).astype(F32)
    ltri = causal.astype(BF16)

    def intra(ig, carry):
        grp = range(DN_INTRA_GROUP)
        rs = [pl.ds((ig * DN_INTRA_GROUP + j) * c, c) for j in grp]
        qc, kc, vc, bc = ([ref[r, :] for r in rs] for ref in (qs, ks, vs, bs))
        gcum = [_dot_sel_lhs(ltri, gs[r, :]) for r in rs]
        decay = [jnp.exp(jnp.where(causal, g - g.T, -jnp.inf)) for g in gcum]
        kb = [kc[j] * bc[j] for j in grp]
        a = [jnp.where(strict, _dot_nt(kb[j], kc[j]) * decay[j], 0.0) for j in grp]
        tinv = [eye - a[j] for j in grp]
        pw = [_hi_lo(x) for x in a]
        for _ in range(int(math.log2(c)) - 1):
            pw = [_hi_lo(_dot_hl(x, x)) for x in pw]
            tinv = [tinv[j] + _dot_hl(_hi_lo(tinv[j]), pw[j]) for j in grp]
        eg = [jnp.exp(g) for g in gcum]
        uw = [_bdot(tinv[j], jnp.concatenate([vc[j] * bc[j], kb[j] * eg[j]], axis=1)) for j in grp]
        qk = [_dot_nt(qc[j], kc[j]) * decay[j] for j in grp]
        glast = [g[c - 1:c, :] for g in gcum]
        k_end = [kc[j] * jnp.exp(glast[j] - gcum[j]) for j in grp]
        qk_uw = [_bdot(qk[j], uw[j]) for j in grp]
        ke_uw = [_dot_tn(k_end[j], uw[j]) for j in grp]
        for j in grp:
            r = rs[j]
            qs[r, :] = qc[j] * eg[j] - qk_uw[j][:, LANES:]
            qks[r, :] = qk_uw[j][:, :LANES]
            ws[r, :] = ke_uw[j][:, LANES:]
            us[r, :] = ke_uw[j][:, :LANES]
            gls[pl.ds(ig * DN_INTRA_GROUP + j, 1), :] = jnp.exp(glast[j])
        return carry

    for ig in range(n_chunks // DN_INTRA_GROUP):
        intra(ig, 0)

    state = jnp.zeros((DN_HEAD_DIM, DN_HEAD_DIM), F32)
    for ic in range(n_chunks):
        r = pl.ds(ic * c, c)
        os_[r, :] = _bdot(qs[r, :], state) + qks[r, :]
        state = state * gls[pl.ds(ic, 1), :] - _bdot(ws[r, :], state) + us[r, :]

    z = z_ref[...].astype(F32)
    o_ref[...] = (_rms(os_[...], onorm_ref[...]) * (z * jax.nn.sigmoid(z))).astype(BF16)


def _gdn(p, p_small, conv_w, a_log, dt_bias, out_norm, batch, seq):
    row_blk = lambda col0: pl.BlockSpec((seq, LANES), lambda b, h, col0=col0: (b, col0 + h))
    conv_blk = lambda col0: pl.BlockSpec((DN_CONV, LANES), lambda b, h, col0=col0: (0, col0 + h))
    smem = pl.BlockSpec(memory_space=pltpu.SMEM)
    return pl.pallas_call(
        _gdn_kernel,
        grid=(batch, DN_HEADS),
        in_specs=[smem, smem,
                  row_blk(COL_Q), row_blk(COL_K), row_blk(COL_V), row_blk(COL_Z),
                  pl.BlockSpec((seq, LANES), lambda b, h: (b, 0)),
                  conv_blk(COL_Q), conv_blk(COL_K), conv_blk(COL_V),
                  pl.BlockSpec((1, LANES), lambda b, h: (0, 0))],
        out_specs=pl.BlockSpec((seq, LANES), lambda b, h: (b, h)),
        out_shape=jax.ShapeDtypeStruct((batch * seq, DN_W), BF16),
        scratch_shapes=([pltpu.VMEM((seq, LANES), F32) for _ in range(9)]
                        + [pltpu.VMEM((seq // DN_CHUNK, LANES), F32)]),
        compiler_params=_cparams(2),
        name="gdn",
    )(a_log, dt_bias, p, p, p, p, p_small, conv_w, conv_w, conv_w, out_norm)


def _gelu_tanh(x):
    return 0.5 * x * (1.0 + jnp.tanh(math.sqrt(2.0 / math.pi) * (x + 0.044715 * (x * x * x))))


def _compress_one(t_ref, pos_ref, w1_ref, w2_ref):
    t = t_ref[0].astype(F32)
    n = t.shape[0]
    y1 = _bdot(t + pos_ref[0], w1_ref[0])
    y2 = _bdot(t + pos_ref[1], w1_ref[1])
    hid = _gelu_tanh(y1 + pltpu.roll(y2, n - 1, axis=0))
    return _bdot(hid, w2_ref[...])


def _compress_kernel(tk_ref, tv_ref, pk_ref, pv_ref, w1k_ref, w2k_ref, w1v_ref, w2v_ref, kn_ref,
                     kc_ref, vc_ref):
    d = NSA_HEAD_DIM
    kc = _compress_one(tk_ref, pk_ref, w1k_ref, w2k_ref)
    kc_ref[0] = jnp.concatenate([_rms(kc[:, g * d:(g + 1) * d], kn_ref[...]) for g in range(NSA_GROUPS)],
                                axis=1)
    vc_ref[0] = _compress_one(tv_ref, pv_ref, w1v_ref, w2v_ref)


def _expand_compress_weights(pos, w1, w2):
    g, d, h = NSA_GROUPS, NSA_HEAD_DIM, CMP_HIDDEN
    eye = jnp.eye(g, dtype=F32)
    w1e = jnp.einsum('aldh,gk->algdkh', w1.reshape(2, CMP_STRIDE, d, h), eye)
    w1e = w1e.reshape(2, CMP_STRIDE * g * d, g * h).astype(BF16)
    w2e = jnp.einsum('hd,gk->ghkd', w2, eye).reshape(g * h, g * d).astype(BF16)
    pose = jnp.broadcast_to(pos.reshape(2, CMP_STRIDE, 1, d), (2, CMP_STRIDE, g, d))
    return pose.reshape(2, 1, CMP_STRIDE * g * d), w1e, w2e


def _compress(tk, tv, pos_k, w1k, w2k, pos_v, w1v, w2v, kn_cmp):
    batch, n, width = tk.shape
    tok = pl.BlockSpec((1, n, width), lambda b: (b, 0, 0))
    full = lambda a: pl.BlockSpec(a.shape, lambda b: (0,) * a.ndim, pipeline_mode=pl.Buffered(1))
    out = pl.BlockSpec((1, n, NSA_KV_W), lambda b: (b, 0, 0))
    return pl.pallas_call(
        _compress_kernel,
        grid=(batch,),
        in_specs=[tok, tok, full(pos_k), full(pos_v), full(w1k), full(w2k), full(w1v), full(w2v),
                  full(kn_cmp)],
        out_specs=[out, out],
        out_shape=[jax.ShapeDtypeStruct((batch, n, NSA_KV_W), F32)] * 2,
        compiler_params=_cparams(1),
        name="nsa_compress",
    )(tk, tv, pos_k, pos_v, w1k, w2k, w1v, w2v, kn_cmp)


def _nsa_kernel(q_ref, ks_ref, vs_ref, kw_ref, vw_ref, kc_ref, vc_ref, sm_ref, qn_ref, kns_ref,
                knw_ref, ovt_ref, gsel_ref, hsum_ref, o_ref, ksa, vsa, kwa, vwa, kca, vca, m_slc, acc_slc,
                m_win, acc_win, tile_used):
    it = pl.program_id(1)
    tile = ATT_TILE
    s_len = ks_ref.shape[0]
    d = NSA_HEAD_DIM
    rep = NSA_REP
    rows = rep * tile
    n_sel = s_len // SLC_LEN
    n_cend = kc_ref.shape[1]

    @pl.when(it == 0)
    def _():
        rowi = lax.broadcasted_iota(jnp.int32, (s_len, d), 0)
        xl = lax.broadcasted_iota(jnp.int32, (s_len, d), 1)
        in_tile = (rowi & (tile - 1)).astype(F32)
        tile_start = (rowi - (rowi & (tile - 1))).astype(F32)
        pos_cols = jnp.where((xl == X_SLOPE) | (xl == X_SLOPE + 1), in_tile,
                             jnp.where((xl == X_SLOPE + 2) | (xl == X_SLOPE + 3), tile_start, 0.0))
        blk_cols = jnp.where(xl == lax.shift_right_logical(rowi, int(math.log2(SLC_LEN))), 1.0, 0.0)
        ci = lax.broadcasted_iota(jnp.int32, (n_cend, d), 0)
        cl = lax.broadcasted_iota(jnp.int32, (n_cend, d), 1)
        cend_cols = jnp.where((cl == X_SLOPE) | (cl == X_SLOPE + 1), (CMP_STRIDE * ci).astype(F32), 0.0)
        ones_s = jnp.ones((s_len, LANES), F32)
        ones_c = jnp.ones((n_cend, LANES), F32)
        for g in range(NSA_GROUPS):
            sl = slice(g * d, (g + 1) * d)
            ksa[g] = jnp.concatenate([_rms(ks_ref[:, sl].astype(F32), kns_ref[...]), pos_cols + blk_cols],
                                     axis=1).astype(BF16)
            kwa[g] = jnp.concatenate([_rms(kw_ref[:, sl].astype(F32), knw_ref[...]), pos_cols],
                                     axis=1).astype(BF16)
            kca[g] = jnp.concatenate([kc_ref[0, :, sl], cend_cols], axis=1).astype(BF16)
            vs_g, vw_g = vs_ref[:, sl].astype(F32), vw_ref[:, sl].astype(F32)
            vsa[g] = jnp.concatenate([vs_g, vs_g, ones_s], axis=1).astype(BF16)
            vwa[g] = jnp.concatenate([vw_g, vw_g, ones_s], axis=1).astype(BF16)
            vca[g] = jnp.concatenate([vc_ref[0, :, sl], vc_ref[0, :, sl], ones_c], axis=1).astype(BF16)

    t0 = it * tile
    groups = range(NSA_GROUPS)
    row_r =lax.shift_right_logical(lax.broadcasted_iota(jnp.int32, (rows, d), 0),
                                    int(math.log2(tile)))
    tpos = t0 + (lax.broadcasted_iota(jnp.int32, (rows, LANES), 0) & (tile - 1))
    xl = lax.broadcasted_iota(jnp.int32, (rows, d), 1)
    col = lax.broadcasted_iota(jnp.int32, (rows, tile), 1)
    row_tt = lax.broadcasted_iota(jnp.int32, (rows, tile), 0) & (tile - 1)
    lower = col <= row_tt
    upper = col > row_tt
    cend = CMP_STRIDE * lax.broadcasted_iota(jnp.int32, (1, LANES), 1) + (CMP_LEN - 1)
    cmp_ok = tpos >= cend
    any_valid = (tpos >= CMP_LEN - 1).astype(F32)

    def attend(items):
        def scores(x, rb):
            r = slice(rb * ATT_ROW_BLOCK, (rb + 1) * ATT_ROW_BLOCK)
            s = _dot_nt(x[0][r], x[1])
            if x[3] is not None:
                s = jnp.where(x[3][r], s, NEG)
            return s, jnp.max(s, axis=-1, keepdims=True)

        def update(x, rb, s, s_max):
            r = slice(rb * ATT_ROW_BLOCK, (rb + 1) * ATT_ROW_BLOCK)
            g = x[6]
            m_old = x[4][g, r, :]
            m_new = jnp.maximum(m_old, jnp.broadcast_to(s_max, m_old.shape))
            alpha = jnp.exp2(m_old - m_new)
            p = jnp.exp2(s - jnp.concatenate([m_new] * (tile // LANES), axis=1))
            x[5][g, r, :] = jnp.concatenate([alpha, alpha], axis=1) * x[5][g, r, :] + _bdot(p, x[2])
            x[4][g, r, :] = m_new

        pending = None
        for x in items:
            for rb in range(rows // ATT_ROW_BLOCK):
                cur = (x, rb) + scores(x, rb)
                if pending is not None:
                    update(*pending)
                pending = cur
        update(*pending)

    log2e = math.log2(math.e)
    qf = q_ref[...].astype(F32)
    hw = hsum_ref.shape[0]
    sq_hi, sq_lo = _hi_lo(qf * qf)
    ssq = jnp.concatenate(
        [jnp.dot(sq_hi[:, j:j + hw], hsum_ref[...], preferred_element_type=F32)
         + jnp.dot(sq_lo[:, j:j + hw], hsum_ref[...], preferred_element_type=F32)
         for j in range(0, NSA_W, hw)], axis=1)
    qn = qf * lax.rsqrt(ssq * (1.0 / d) + EPS) * (qn_ref[...] * (log2e * d ** -0.5))
    q64, q_x, q_aug = [], [], []
    for g in groups:
        slopes = [log2e * 2.0 ** (-8.0 * (g * rep + r + 1) / NSA_HEADS) for r in range(rep)]
        sl = jnp.full((rows, d), slopes[rep - 1], F32)
        for r in range(rep - 2, -1, -1):
            sl = jnp.where(row_r == r, slopes[r], sl)
        sl_hi = sl.astype(BF16).astype(F32)
        q_x.append(jnp.where((xl == X_SLOPE) | (xl == X_SLOPE + 2), sl_hi,
                             jnp.where((xl == X_SLOPE + 1) | (xl == X_SLOPE + 3), sl - sl_hi, 0.0)))
        q64.append(jnp.concatenate(
            [qn[:, (g * rep + r) * d:(g * rep + r + 1) * d] for r in range(rep)], axis=0))
        q_aug.append(jnp.concatenate([q64[g], q_x[g]], axis=1).astype(BF16))

    s_c = [jnp.where(cmp_ok, _dot_nt(q_aug[g], kca[g]), NEG) for g in groups]
    e_c = [jnp.exp2(s - jnp.max(s, axis=-1, keepdims=True)) for s in s_c]
    pv_c = [_bdot(e_c[g], vca[g]) for g in groups]
    w_c = [any_valid / pv[:, LANES:] for pv in pv_c]
    p_c = [e_c[g] * w_c[g] for g in groups]
    o_cmp = [pv_c[g][:, :LANES] * w_c[g] for g in groups]
    p_sum = [sum(p[r * tile:(r + 1) * tile] for r in range(rep)) for p in p_c]

    jj = lax.broadcasted_iota(jnp.int32, (X_SLOPE, tile), 0)
    blk_t = lax.shift_right_logical(t0 + lax.broadcasted_iota(jnp.int32, (1, tile), 1), int(math.log2(SLC_LEN)))
    imp = [sum(_dot_nt(ovt_ref[...], t) for t in _split3(ps))[:X_SLOPE] for ps in p_sum]
    imp = [jnp.where(jj <= blk_t, x, NEG) for x in imp]
    imp = [jnp.where((jj == blk_t) | (jj == 0), FORCE, x) for x in imp]
    imp = [jnp.where(jj < n_sel, x, -3e38) for x in imp]
    rank = [jnp.zeros((X_SLOPE, tile), F32) for _ in groups]
    for j in range(n_sel):
        for g in groups:
            row = imp[g][j:j + 1, :]
            ge = jnp.where(row >= imp[g], 1.0, 0.0)
            gt = jnp.where(row > imp[g], 1.0, 0.0)
            rank[g] = rank[g] + jnp.where(jj > j, ge, gt)
    k_sel = float(min(SLC_TOPK, n_sel))
    bias_t = [jnp.where((rk < k_sel) | (jj >= n_sel), 0.0, NEG) for rk in rank]
    zero_rows = jnp.zeros((LANES - X_SLOPE, tile), F32)
    sel_bias = [jnp.concatenate([bt, zero_rows], axis=0).T[:, :d] for bt in bias_t]
    q_sel = [jnp.concatenate([q64[g], q_x[g] + jnp.concatenate([sel_bias[g]] * rep, axis=0)],
                             axis=1).astype(BF16) for g in groups]

    for g in groups:
        m_slc[g] = jnp.full((rows, LANES), -jnp.inf, F32)
        m_win[g] = jnp.full((rows, LANES), -jnp.inf, F32)
        acc_slc[g] = jnp.zeros((rows, 2 * LANES), F32)
        acc_win[g] = jnp.zeros((rows, 2 * LANES), F32)

    r_diag = pl.ds(pl.multiple_of(t0, tile), tile)
    attend([(q_sel[g], ksa[g, r_diag, :], vsa[g, r_diag, :], lower, m_slc, acc_slc, g) for g in groups])
    attend([(q_aug[g], kwa[g, r_diag, :], vwa[g, r_diag, :], lower, m_win, acc_win, g) for g in groups])

    @pl.when(it > 0)
    def _():
        r_prev = pl.ds(pl.multiple_of(t0 - tile, tile), tile)
        attend([(q_aug[g], kwa[g, r_prev, :], vwa[g, r_prev, :], upper, m_win, acc_win, g) for g in groups])

    picked = functools.reduce(jnp.maximum, [jnp.where(rk < k_sel, 1.0, 0.0) for rk in rank])
    blocks_per_tile = tile // SLC_LEN
    for kt in range(s_len // tile):
        tile_used[kt] = jnp.max(picked[kt * blocks_per_tile:(kt + 1) * blocks_per_tile, :]).astype(jnp.int32)

    def slc_tile(kt, carry):
        @pl.when(tile_used[kt] > 0)
        def _():
            r = pl.ds(pl.multiple_of(kt * tile, tile), tile)
            attend([(q_sel[g], ksa[g, r, :], vsa[g, r, :], None, m_slc, acc_slc, g) for g in groups])

        return carry

    lax.fori_loop(0, it, slc_tile, 0)

    g_hi, g_lo = _hi_lo(jax.nn.sigmoid(sm_ref[...]))
    gate_w = (jnp.dot(g_hi, gsel_ref[...], preferred_element_type=F32)
              + jnp.dot(g_lo, gsel_ref[...], preferred_element_type=F32))
    low_half = lax.broadcasted_iota(jnp.int32, (tile, LANES), 1) < d
    for g in groups:
        o_slc = acc_slc[g][:, :LANES] / acc_slc[g][:, LANES:]
        o_win = acc_win[g][:, :LANES] / acc_win[g][:, LANES:]
        for pr in range(rep // 2):
            pair = g * (rep // 2) + pr
            r_even = slice(2 * pr * tile, (2 * pr + 1) * tile)
            r_odd = slice((2 * pr + 1) * tile, (2 * pr + 2) * tile)
            merged = 0.0
            for br, o in enumerate((o_cmp[g], o_slc, o_win)):
                c0 = (pair * 3 + br) * LANES
                merged = merged + gate_w[:, c0:c0 + LANES] * jnp.where(low_half, o[r_even], o[r_odd])
            o_ref[:, pair * LANES:(pair + 1) * LANES] = merged.astype(BF16)


def _gate_spread():
    n_pairs = NSA_HEADS // 2
    c = np.arange(LANES)[:, None]
    col = np.arange(n_pairs * 3 * LANES)[None, :]
    pair, br, n = col // (3 * LANES), (col // LANES) % 3, col % LANES
    head = 2 * pair + (n >= NSA_HEAD_DIM)
    return jnp.asarray((c == SMALL_GATE + 3 * head + br).astype(np.float32), BF16)


def _head_sum(width=2 * LANES):
    i = np.arange(width)
    return jnp.asarray((i[:, None] // NSA_HEAD_DIM == i[None, :] // NSA_HEAD_DIM).astype(np.float32), BF16)


def _nsa(p, p_small, kc, vc, q_norm, kn_slc, kn_win, ovt, batch, seq):
    gsel, hsum = _gate_spread(), _head_sum()
    q_gain = jnp.tile(q_norm, (1, NSA_HEADS))
    tile = ATT_TILE
    nt = seq // tile
    rows = NSA_REP * tile
    kv = lambda col: pl.BlockSpec((seq, NSA_KV_W), lambda b, i, col=col: (b, col // 2),
                                  pipeline_mode=pl.Buffered(1))
    full = lambda a: pl.BlockSpec(a.shape, lambda b, i: (0,) * a.ndim)
    cmp_blk = pl.BlockSpec((1,) + kc.shape[1:], lambda b, i: (b, 0, 0))
    aug = lambda n: pltpu.VMEM((NSA_GROUPS, n, LANES), BF16)
    val = lambda n: pltpu.VMEM((NSA_GROUPS, n, 2 * LANES), BF16)
    stat = pltpu.VMEM((NSA_GROUPS, rows, LANES), F32)
    acc = pltpu.VMEM((NSA_GROUPS, rows, 2 * LANES), F32)
    return pl.pallas_call(
        _nsa_kernel,
        grid=(batch, nt),
        in_specs=[pl.BlockSpec((tile, NSA_W), lambda b, i: (b * nt + i, COL_NSQ // 8)),
                  kv(COL_KS), kv(COL_VS), kv(COL_KW), kv(COL_VW),
                  cmp_blk, cmp_blk,
                  pl.BlockSpec((tile, LANES), lambda b, i: (b * nt + i, 0)),
                  full(q_gain), full(kn_slc), full(kn_win), full(ovt), full(gsel), full(hsum)],
        out_specs=pl.BlockSpec((tile, NSA_W), lambda b, i: (b * nt + i, 0)),
        out_shape=jax.ShapeDtypeStruct((batch * seq, NSA_W), BF16),
        scratch_shapes=[aug(seq), val(seq), aug(seq), val(seq), aug(kc.shape[1]), val(kc.shape[1]),
                        stat, acc, stat, acc, pltpu.SMEM((nt,), jnp.int32)],
        compiler_params=_cparams(2),
        name="nsa_attention",
    )(p, p, p, p, p, kc, vc, p_small, q_gain, kn_slc, kn_win, ovt, gsel, hsum)


def _mix_kernel(x_ref, odn_ref, ons_ref, gdn_ref, gns_ref, wdn_ref, wns_ref, wout_ref, o_ref):
    y_dn = jnp.dot(odn_ref[...], wdn_ref[...], preferred_element_type=F32)
    y_ns = jnp.dot(ons_ref[...], wns_ref[...], preferred_element_type=F32)
    mix = (jax.nn.sigmoid(gdn_ref[...].astype(F32)) * y_dn
           + jax.nn.sigmoid(gns_ref[...].astype(F32)) * y_ns)
    o_ref[...] = x_ref[...] + jnp.dot(mix.astype(BF16), wout_ref[...], preferred_element_type=F32)


def _mix(x2, o_dn, o_ns, p, w_dn, w_ns, w_out, tm=512):
    t = x2.shape[0]
    row = lambda col: pl.BlockSpec((tm, D_MODEL), lambda i, col=col: (i, col))
    wfull = pl.BlockSpec((D_MODEL, D_MODEL), lambda i: (0, 0))
    return pl.pallas_call(
        _mix_kernel,
        grid=(t // tm,),
        in_specs=[row(0), row(0), row(0), row(COL_GDN // 8), row(COL_GNS // 8), wfull, wfull, wfull],
        out_specs=row(0),
        out_shape=jax.ShapeDtypeStruct((t, D_MODEL), F32),
        compiler_params=_cparams(1),
        name="mix_out",
    )(x2, o_dn, o_ns, p, p, w_dn, w_ns, w_out)


def _ffn_kernel(x_ref, g_ref, wup_ref, wdown_ref, o_ref, h_ref, acc_ref):
    f = pl.program_id(1)

    @pl.when(f == 0)
    def _():
        h_ref[...] = _rms(x_ref[...], g_ref[...]).astype(BF16)
        acc_ref[...] = jnp.zeros_like(acc_ref)

    u = jnp.dot(h_ref[...], wup_ref[...], preferred_element_type=F32)
    u = jnp.square(jnp.maximum(u, 0.0)).astype(BF16)
    acc_ref[...] += jnp.dot(u, wdown_ref[...], preferred_element_type=F32)

    @pl.when(f == pl.num_programs(1) - 1)
    def _():
        o_ref[...] = x_ref[...] + acc_ref[...]


def _ffn(x2, gain, w_up, w_down, tm=1024, tf=1024):
    t = x2.shape[0]
    return pl.pallas_call(
        _ffn_kernel,
        grid=(t // tm, D_FF // tf),
        in_specs=[pl.BlockSpec((tm, D_MODEL), lambda i, f: (i, 0)),
                  pl.BlockSpec((1, D_MODEL), lambda i, f: (0, 0)),
                  pl.BlockSpec((D_MODEL, tf), lambda i, f: (0, f)),
                  pl.BlockSpec((tf, D_MODEL), lambda i, f: (f, 0))],
        out_specs=pl.BlockSpec((tm, D_MODEL), lambda i, f: (i, 0)),
        out_shape=jax.ShapeDtypeStruct((t, D_MODEL), F32),
        scratch_shapes=[pltpu.VMEM((tm, D_MODEL), BF16), pltpu.VMEM((tm, D_MODEL), F32)],
        compiler_params=_cparams(2),
        name="ffn",
    )(x2, gain, w_up, w_down)


def _overlap_matrix_t(seq):
    n_cmp = (seq - CMP_LEN) // CMP_STRIDE + 1
    n_sel = seq // SLC_LEN
    c0 = np.arange(n_cmp)[None, :] * CMP_STRIDE
    j0 = np.arange(n_sel)[:, None] * SLC_LEN
    ov = np.clip(np.minimum(c0 + CMP_LEN, j0 + SLC_LEN) - np.maximum(c0, j0), 0, None) / CMP_LEN
    out = np.zeros((LANES, LANES), np.float32)
    out[:n_sel, :n_cmp] = ov
    return jnp.asarray(out, BF16)


def _regroup_w_in(w):
    o_b = 4 * DN_W
    o_q = o_b + 2 * DN_HEADS
    o_kv = o_q + NSA_W
    o_gate = o_kv + 6 * NSA_KV_W
    o_gdn = o_gate + 3 * NSA_HEADS
    pad = jnp.zeros((w.shape[0], P_WIDTH - w.shape[1]), w.dtype)
    return jnp.concatenate(
        [w[:, :o_b], w[:, o_q:o_kv], w[:, o_gdn:], w[:, o_kv:o_gate], w[:, o_b:o_q],
         w[:, o_gate:o_gdn], pad], axis=1).astype(BF16)


def _tokens16(p, col, batch, seq):
    t = p[:, col * LANES:col * LANES + NSA_KV_W]
    return t.reshape(batch, seq // CMP_STRIDE, CMP_STRIDE * NSA_KV_W)


def kernel(x, norm_mix, w_in, dn_conv, dn_a_log, dn_dt_bias, dn_out_norm, nsa_q_norm, nsa_k_norm_cmp,
           nsa_k_norm_slc, nsa_k_norm_win, cmp_pos_k, cmp_w1_k, cmp_w2_k, cmp_pos_v, cmp_w1_v, cmp_w2_v,
           w_proj_dn, w_proj_nsa, w_out, norm_mlp, w_up, w_down):
    batch, seq, _ = x.shape
    assert seq // CMP_STRIDE == LANES and seq % ATT_TILE == 0 and seq // SLC_LEN <= X_SLOPE
    ovt = _overlap_matrix_t(seq)
    x2 = x.reshape(batch * seq, D_MODEL)
    for l in range(w_in.shape[0]):
        p, p_small = _in_proj(x2, norm_mix[l][None], _regroup_w_in(w_in[l]))
        o_dn = _gdn(p, p_small, dn_conv[l], dn_a_log[l][None], dn_dt_bias[l][None], dn_out_norm[l][None],
                    batch, seq)
        kc, vc = _compress(_tokens16(p, COL_KC, batch, seq), _tokens16(p, COL_VC, batch, seq),
                           *_expand_compress_weights(cmp_pos_k[l], cmp_w1_k[l], cmp_w2_k[l]),
                           *_expand_compress_weights(cmp_pos_v[l], cmp_w1_v[l], cmp_w2_v[l]),
                           nsa_k_norm_cmp[l][None])
        o_ns = _nsa(p, p_small, kc, vc, nsa_q_norm[l][None], nsa_k_norm_slc[l][None], nsa_k_norm_win[l][None],
                    ovt, batch, seq)
        x2 = _mix(x2, o_dn, o_ns, p, w_proj_dn[l].astype(BF16), w_proj_nsa[l].astype(BF16),
                  w_out[l].astype(BF16))
        x2 = _ffn(x2, norm_mlp[l][None], w_up[l].astype(BF16), w_down[l].astype(BF16))
    return x2.reshape(batch, seq, D_MODEL)
```

```python
import functools
import math

import numpy as np
import jax
import jax.numpy as jnp
from jax import lax
from jax.experimental import pallas as pl
from jax.experimental.pallas import tpu as pltpu

F32 = jnp.float32
BF16 = jnp.bfloat16

D_MODEL = 1024
DN_HEADS = 8
DN_HEAD_DIM = 128
DN_W = DN_HEADS * DN_HEAD_DIM
DN_CONV = 4
NSA_HEADS = 16
NSA_GROUPS = 4
NSA_REP = NSA_HEADS // NSA_GROUPS
NSA_HEAD_DIM = 64
NSA_W = NSA_HEADS * NSA_HEAD_DIM
NSA_KV_W = NSA_GROUPS * NSA_HEAD_DIM
CMP_LEN = 32
CMP_STRIDE = 16
CMP_HIDDEN = 2 * NSA_HEAD_DIM
SLC_LEN = 64
SLC_TOPK = 8
WINDOW = 256
D_FF = 4 * D_MODEL
EPS = 1e-6
NEG = -1e30
FORCE = 1e9
LANES = 128

DN_CHUNK = LANES
DN_INTRA_GROUP = 8
ATT_TILE = WINDOW
ATT_ROW_BLOCK = 128

COL_Q, COL_K, COL_V, COL_Z = 0, 8, 16, 24
COL_NSQ, COL_GDN, COL_GNS = 32, 40, 48
COL_KC, COL_VC, COL_KS, COL_VS, COL_KW, COL_VW = 56, 58, 60, 62, 64, 66
COL_SMALL = 68
P_WIDTH = 72 * LANES
SMALL_B, SMALL_A, SMALL_GATE = 0, DN_HEADS, 2 * DN_HEADS

X_SLOPE = SLC_TOPK * 4

VMEM_LIMIT = 56 * 1024 * 1024


def _cparams(n_axes):
    return pltpu.CompilerParams(dimension_semantics=("arbitrary",) * n_axes,
                                vmem_limit_bytes=VMEM_LIMIT)


def _bdot(a, b):
    return jnp.dot(a.astype(BF16), b.astype(BF16), preferred_element_type=F32)


def _dot_nt(a, b):
    return lax.dot_general(a.astype(BF16), b.astype(BF16), (((1,), (1,)), ((), ())),
                           preferred_element_type=F32)


def _dot_tn(a, b):
    return lax.dot_general(a.astype(BF16), b.astype(BF16), (((0,), (0,)), ((), ())),
                           preferred_element_type=F32)


def _split3(a):
    a1 = a.astype(BF16)
    r = a - a1.astype(F32)
    a2 = r.astype(BF16)
    a3 = (r - a2.astype(F32)).astype(BF16)
    return a1, a2, a3


def _hi_lo(a):
    hi = a.astype(BF16)
    return hi, (a - hi.astype(F32)).astype(BF16)


def _dot_hl(x, y):
    (xh, xl), (yh, yl) = x, y
    return (jnp.dot(jnp.concatenate([xh, xl], axis=1), jnp.concatenate([yh, yh], axis=0),
                    preferred_element_type=F32)
            + jnp.dot(xh, yl, preferred_element_type=F32))


def _dot_sel_rhs(a, sel):
    return sum(jnp.dot(t, sel, preferred_element_type=F32) for t in _split3(a))


def _dot_sel_lhs(sel, b):
    return sum(jnp.dot(sel, t, preferred_element_type=F32) for t in _split3(b))


def _rms(x, gain):
    return x * lax.rsqrt(jnp.mean(x * x, axis=-1, keepdims=True) + EPS) * gain


def _inproj_kernel(x_ref, g_ref, w_ref, ws_ref, o_ref, os_ref, h_ref):
    @pl.when(pl.program_id(1) == 0)
    def _():
        h_ref[...] = _rms(x_ref[...], g_ref[...]).astype(BF16)
        os_ref[...] = jnp.dot(h_ref[...], ws_ref[...], preferred_element_type=F32)

    o_ref[...] = jnp.dot(h_ref[...], w_ref[...], preferred_element_type=F32).astype(BF16)


def _in_proj(x2, gain, w_perm, tm=1024, tn=3072):
    t = x2.shape[0]
    return pl.pallas_call(
        _inproj_kernel,
        grid=(t // tm, P_WIDTH // tn),
        in_specs=[pl.BlockSpec((tm, D_MODEL), lambda i, j: (i, 0)),
                  pl.BlockSpec((1, D_MODEL), lambda i, j: (0, 0)),
                  pl.BlockSpec((D_MODEL, tn), lambda i, j: (0, j)),
                  pl.BlockSpec((D_MODEL, LANES), lambda i, j: (0, COL_SMALL))],
        out_specs=[pl.BlockSpec((tm, tn), lambda i, j: (i, j)),
                   pl.BlockSpec((tm, LANES), lambda i, j: (i, 0))],
        out_shape=[jax.ShapeDtypeStruct((t, P_WIDTH), BF16), jax.ShapeDtypeStruct((t, LANES), F32)],
        scratch_shapes=[pltpu.VMEM((tm, D_MODEL), BF16)],
        compiler_params=_cparams(2),
        name="in_proj",
    )(x2, gain, w_perm, w_perm)


def _conv_silu(x, w):
    rows = lax.broadcasted_iota(jnp.int32, x.shape, 0)
    y = x * w[DN_CONV - 1:DN_CONV, :]
    for j in range(DN_CONV - 1):
        sh = DN_CONV - 1 - j
        xs = jnp.where(rows >= sh, pltpu.roll(x, sh, axis=0), 0.0)
        y = y + xs * w[j:j + 1, :]
    return y * jax.nn.sigmoid(y)


def _gdn_kernel(alog_ref, dtb_ref, q_ref, k_ref, v_ref, z_ref, sm_ref, cq_ref, ck_ref, cv_ref,
                onorm_ref, o_ref, qs, ks, vs, gs, bs, us, ws, qks, os_, gls):
    h = pl.program_id(1)
    s_len = q_ref.shape[0]
    c = DN_CHUNK
    n_chunks = s_len // c

    q = _conv_silu(q_ref[...].astype(F32), cq_ref[...])
    k = _conv_silu(k_ref[...].astype(F32), ck_ref[...])
    v = _conv_silu(v_ref[...].astype(F32), cv_ref[...])
    qs[...] = q * lax.rsqrt(jnp.sum(q * q, axis=-1, keepdims=True) + EPS) * (DN_HEAD_DIM ** -0.5)
    ks[...] = k * lax.rsqrt(jnp.sum(k * k, axis=-1, keepdims=True) + EPS)
    vs[...] = v

    kk = lax.broadcasted_iota(jnp.int32, (LANES, 2 * LANES), 0)
    nn = lax.broadcasted_iota(jnp.int32, (LANES, 2 * LANES), 1)
    onehot = (kk == jnp.where(nn < LANES, SMALL_B + h, SMALL_A + h)).astype(BF16)
    ba = _dot_sel_rhs(sm_ref[...], onehot)
    neg_a = -jnp.exp(jnp.full((1, LANES), alog_ref[0, h], F32))
    dtb = jnp.full((1, LANES), dtb_ref[0, h], F32)
    bs[...] = jax.nn.sigmoid(ba[:, :LANES])
    gs[...] = neg_a * jax.nn.softplus(ba[:, LANES:] + dtb)

    ri = lax.broadcasted_iota(jnp.int32, (c, c), 0)
    ci = lax.broadcasted_iota(jnp.int32, (c, c), 1)
    causal = ri >= ci
    strict = ri > ci
    eye = (ri == ci).astype(F32)
    ltri = causal.astype(BF16)

    def intra(ig, carry):
        grp = range(DN_INTRA_GROUP)
        rs = [pl.ds((ig * DN_INTRA_GROUP + j) * c, c) for j in grp]
        qc, kc, vc, bc = ([ref[r, :] for r in rs] for ref in (qs, ks, vs, bs))
        gcum = [_dot_sel_lhs(ltri, gs[r, :]) for r in rs]
        decay = [jnp.exp(jnp.where(causal, g - g.T, -jnp.inf)) for g in gcum]
        kb = [kc[j] * bc[j] for j in grp]
        a = [jnp.where(strict, _dot_nt(kb[j], kc[j]) * decay[j], 0.0) for j in grp]
        tinv = [eye - a[j] for j in grp]
        pw = [_hi_lo(x) for x in a]
        for _ in range(int(math.log2(c)) - 1):
            pw = [_hi_lo(_dot_hl(x, x)) for x in pw]
            tinv = [tinv[j] + _dot_hl(_hi_lo(tinv[j]), pw[j]) for j in grp]
        eg = [jnp.exp(g) for g in gcum]
        uw = [_bdot(tinv[j], jnp.concatenate([vc[j] * bc[j], kb[j] * eg[j]], axis=1)) for j in grp]
        qk = [_dot_nt(qc[j], kc[j]) * decay[j] for j in grp]
        glast = [g[c - 1:c, :] for g in gcum]
        k_end = [kc[j] * jnp.exp(glast[j] - gcum[j]) for j in grp]
        qk_uw = [_bdot(qk[j], uw[j]) for j in grp]
        ke_uw = [_dot_tn(k_end[j], uw[j]) for j in grp]
        for j in grp:
            r = rs[j]
            qs[r, :] = qc[j] * eg[j] - qk_uw[j][:, LANES:]
            qks[r, :] = qk_uw[j][:, :LANES]
            ws[r, :] = ke_uw[j][:, LANES:]
            us[r, :] = ke_uw[j][:, :LANES]
            gls[pl.ds(ig * DN_INTRA_GROUP + j, 1), :] = jnp.exp(glast[j])
        return carry

    for ig in range(n_chunks // DN_INTRA_GROUP):
        intra(ig, 0)

    state = jnp.zeros((DN_HEAD_DIM, DN_HEAD_DIM), F32)
    for ic in range(n_chunks):
        r = pl.ds(ic * c, c)
        os_[r, :] = _bdot(qs[r, :], state) + qks[r, :]
        state = state * gls[pl.ds(ic, 1), :] - _bdot(ws[r, :], state) + us[r, :]

    z = z_ref[...].astype(F32)
    o_ref[...] = (_rms(os_[...], onorm_ref[...]) * (z * jax.nn.sigmoid(z))).astype(BF16)


def _gdn(p, p_small, conv_w, a_log, dt_bias, out_norm, batch, seq):
    row_blk = lambda col0: pl.BlockSpec((seq, LANES), lambda b, h, col0=col0: (b, col0 + h))
    conv_blk = lambda col0: pl.BlockSpec((DN_CONV, LANES), lambda b, h, col0=col0: (0, col0 + h))
    smem = pl.BlockSpec(memory_space=pltpu.SMEM)
    return pl.pallas_call(
        _gdn_kernel,
        grid=(batch, DN_HEADS),
        in_specs=[smem, smem,
                  row_blk(COL_Q), row_blk(COL_K), row_blk(COL_V), row_blk(COL_Z),
                  pl.BlockSpec((seq, LANES), lambda b, h: (b, 0)),
                  conv_blk(COL_Q), conv_blk(COL_K), conv_blk(COL_V),
                  pl.BlockSpec((1, LANES), lambda b, h: (0, 0))],
        out_specs=pl.BlockSpec((seq, LANES), lambda b, h: (b, h)),
        out_shape=jax.ShapeDtypeStruct((batch * seq, DN_W), BF16),
        scratch_shapes=([pltpu.VMEM((seq, LANES), F32) for _ in range(9)]
                        + [pltpu.VMEM((seq // DN_CHUNK, LANES), F32)]),
        compiler_params=_cparams(2),
        name="gdn",
    )(a_log, dt_bias, p, p, p, p, p_small, conv_w, conv_w, conv_w, out_norm)


def _gelu_tanh(x):
    return 0.5 * x * (1.0 + jnp.tanh(math.sqrt(2.0 / math.pi) * (x + 0.044715 * (x * x * x))))


def _compress_one(t_ref, pos_ref, w1_ref, w2_ref):
    t = t_ref[0].astype(F32)
    n = t.shape[0]
    y1 = _bdot(t + pos_ref[0], w1_ref[0])
    y2 = _bdot(t + pos_ref[1], w1_ref[1])
    hid = _gelu_tanh(y1 + pltpu.roll(y2, n - 1, axis=0))
    return _bdot(hid, w2_ref[...])


def _compress_kernel(tk_ref, tv_ref, pk_ref, pv_ref, w1k_ref, w2k_ref, w1v_ref, w2v_ref, kn_ref,
                     kc_ref, vc_ref):
    d = NSA_HEAD_DIM
    kc = _compress_one(tk_ref, pk_ref, w1k_ref, w2k_ref)
    kc_ref[0] = jnp.concatenate([_rms(kc[:, g * d:(g + 1) * d], kn_ref[...]) for g in range(NSA_GROUPS)],
                                axis=1)
    vc_ref[0] = _compress_one(tv_ref, pv_ref, w1v_ref, w2v_ref)


def _expand_compress_weights(pos, w1, w2):
    g, d, h = NSA_GROUPS, NSA_HEAD_DIM, CMP_HIDDEN
    eye = jnp.eye(g, dtype=F32)
    w1e = jnp.einsum('aldh,gk->algdkh', w1.reshape(2, CMP_STRIDE, d, h), eye)
    w1e = w1e.reshape(2, CMP_STRIDE * g * d, g * h).astype(BF16)
    w2e = jnp.einsum('hd,gk->ghkd', w2, eye).reshape(g * h, g * d).astype(BF16)
    pose = jnp.broadcast_to(pos.reshape(2, CMP_STRIDE, 1, d), (2, CMP_STRIDE, g, d))
    return pose.reshape(2, 1, CMP_STRIDE * g * d), w1e, w2e


def _compress(tk, tv, pos_k, w1k, w2k, pos_v, w1v, w2v, kn_cmp):
    batch, n, width = tk.shape
    tok = pl.BlockSpec((1, n, width), lambda b: (b, 0, 0))
    full = lambda a: pl.BlockSpec(a.shape, lambda b: (0,) * a.ndim, pipeline_mode=pl.Buffered(1))
    out = pl.BlockSpec((1, n, NSA_KV_W), lambda b: (b, 0, 0))
    return pl.pallas_call(
        _compress_kernel,
        grid=(batch,),
        in_specs=[tok, tok, full(pos_k), full(pos_v), full(w1k), full(w2k), full(w1v), full(w2v),
                  full(kn_cmp)],
        out_specs=[out, out],
        out_shape=[jax.ShapeDtypeStruct((batch, n, NSA_KV_W), F32)] * 2,
        compiler_params=_cparams(1),
        name="nsa_compress",
    )(tk, tv, pos_k, pos_v, w1k, w2k, w1v, w2v, kn_cmp)


def _nsa_kernel(q_ref, ks_ref, vs_ref, kw_ref, vw_ref, kc_ref, vc_ref, sm_ref, qn_ref, kns_ref,
                knw_ref, ovt_ref, gsel_ref, hsum_ref, o_ref, ksa, vsa, kwa, vwa, kca, vca, m_slc, acc_slc,
                m_win, acc_win, tile_used):
    it = pl.program_id(1)
    tile = ATT_TILE
    s_len = ks_ref.shape[0]
    d = NSA_HEAD_DIM
    rep = NSA_REP
    rows = rep * tile
    n_sel = s_len // SLC_LEN
    n_cend = kc_ref.shape[1]

    @pl.when(it == 0)
    def _():
        rowi = lax.broadcasted_iota(jnp.int32, (s_len, d), 0)
        xl = lax.broadcasted_iota(jnp.int32, (s_len, d), 1)
        in_tile = (rowi & (tile - 1)).astype(F32)
        tile_start = (rowi - (rowi & (tile - 1))).astype(F32)
        pos_cols = jnp.where((xl == X_SLOPE) | (xl == X_SLOPE + 1), in_tile,
                             jnp.where((xl == X_SLOPE + 2) | (xl == X_SLOPE + 3), tile_start, 0.0))
        blk_cols = jnp.where(xl == lax.shift_right_logical(rowi, int(math.log2(SLC_LEN))), 1.0, 0.0)
        ci = lax.broadcasted_iota(jnp.int32, (n_cend, d), 0)
        cl = lax.broadcasted_iota(jnp.int32, (n_cend, d), 1)
        cend_cols = jnp.where((cl == X_SLOPE) | (cl == X_SLOPE + 1), (CMP_STRIDE * ci).astype(F32), 0.0)
        ones_s = jnp.ones((s_len, LANES), F32)
        ones_c = jnp.ones((n_cend, LANES), F32)
        for g in range(NSA_GROUPS):
            sl = slice(g * d, (g + 1) * d)
            ksa[g] = jnp.concatenate([_rms(ks_ref[:, sl].astype(F32), kns_ref[...]), pos_cols + blk_cols],
                                     axis=1).astype(BF16)
            kwa[g] = jnp.concatenate([_rms(kw_ref[:, sl].astype(F32), knw_ref[...]), pos_cols],
                                     axis=1).astype(BF16)
            kca[g] = jnp.concatenate([kc_ref[0, :, sl], cend_cols], axis=1).astype(BF16)
            vs_g, vw_g = vs_ref[:, sl].astype(F32), vw_ref[:, sl].astype(F32)
            vsa[g] = jnp.concatenate([vs_g, vs_g, ones_s], axis=1).astype(BF16)
            vwa[g] = jnp.concatenate([vw_g, vw_g, ones_s], axis=1).astype(BF16)
            vca[g] = jnp.concatenate([vc_ref[0, :, sl], vc_ref[0, :, sl], ones_c], axis=1).astype(BF16)

    t0 = it * tile
    groups = range(NSA_GROUPS)
    row_r =lax.shift_right_logical(lax.broadcasted_iota(jnp.int32, (rows, d), 0),
                                    int(math.log2(tile)))
    tpos = t0 + (lax.broadcasted_iota(jnp.int32, (rows, LANES), 0) & (tile - 1))
    xl = lax.broadcasted_iota(jnp.int32, (rows, d), 1)
    col = lax.broadcasted_iota(jnp.int32, (rows, tile), 1)
    row_tt = lax.broadcasted_iota(jnp.int32, (rows, tile), 0) & (tile - 1)
    lower = col <= row_tt
    upper = col > row_tt
    cend = CMP_STRIDE * lax.broadcasted_iota(jnp.int32, (1, LANES), 1) + (CMP_LEN - 1)
    cmp_ok = tpos >= cend
    any_valid = (tpos >= CMP_LEN - 1).astype(F32)

    def attend(items):
        def scores(x, rb):
            r = slice(rb * ATT_ROW_BLOCK, (rb + 1) * ATT_ROW_BLOCK)
            s = _dot_nt(x[0][r], x[1])
            if x[3] is not None:
                s = jnp.where(x[3][r], s, NEG)
            return s, jnp.max(s, axis=-1, keepdims=True)

        def update(x, rb, s, s_max):
            r = slice(rb * ATT_ROW_BLOCK, (rb + 1) * ATT_ROW_BLOCK)
            g = x[6]
            m_old = x[4][g, r, :]
            m_new = jnp.maximum(m_old, jnp.broadcast_to(s_max, m_old.shape))
            alpha = jnp.exp2(m_old - m_new)
            p = jnp.exp2(s - jnp.concatenate([m_new] * (tile // LANES), axis=1))
            x[5][g, r, :] = jnp.concatenate([alpha, alpha], axis=1) * x[5][g, r, :] + _bdot(p, x[2])
            x[4][g, r, :] = m_new

        pending = None
        for x in items:
            for rb in range(rows // ATT_ROW_BLOCK):
                cur = (x, rb) + scores(x, rb)
                if pending is not None:
                    update(*pending)
                pending = cur
        update(*pending)

    log2e = math.log2(math.e)
    qf = q_ref[...].astype(F32)
    hw = hsum_ref.shape[0]
    sq_hi, sq_lo = _hi_lo(qf * qf)
    ssq = jnp.concatenate(
        [jnp.dot(sq_hi[:, j:j + hw], hsum_ref[...], preferred_element_type=F32)
         + jnp.dot(sq_lo[:, j:j + hw], hsum_ref[...], preferred_element_type=F32)
         for j in range(0, NSA_W, hw)], axis=1)
    qn = qf * lax.rsqrt(ssq * (1.0 / d) + EPS) * (qn_ref[...] * (log2e * d ** -0.5))
    q64, q_x, q_aug = [], [], []
    for g in groups:
        slopes = [log2e * 2.0 ** (-8.0 * (g * rep + r + 1) / NSA_HEADS) for r in range(rep)]
        sl = jnp.full((rows, d), slopes[rep - 1], F32)
        for r in range(rep - 2, -1, -1):
            sl = jnp.where(row_r == r, slopes[r], sl)
        sl_hi = sl.astype(BF16).astype(F32)
        q_x.append(jnp.where((xl == X_SLOPE) | (xl == X_SLOPE + 2), sl_hi,
                             jnp.where((xl == X_SLOPE + 1) | (xl == X_SLOPE + 3), sl - sl_hi, 0.0)))
        q64.append(jnp.concatenate(
            [qn[:, (g * rep + r) * d:(g * rep + r + 1) * d] for r in range(rep)], axis=0))
        q_aug.append(jnp.concatenate([q64[g], q_x[g]], axis=1).astype(BF16))

    s_c = [jnp.where(cmp_ok, _dot_nt(q_aug[g], kca[g]), NEG) for g in groups]
    e_c = [jnp.exp2(s - jnp.max(s, axis=-1, keepdims=True)) for s in s_c]
    pv_c = [_bdot(e_c[g], vca[g]) for g in groups]
    w_c = [any_valid / pv[:, LANES:] for pv in pv_c]
    p_c = [e_c[g] * w_c[g] for g in groups]
    o_cmp = [pv_c[g][:, :LANES] * w_c[g] for g in groups]
    p_sum = [sum(p[r * tile:(r + 1) * tile] for r in range(rep)) for p in p_c]

    jj = lax.broadcasted_iota(jnp.int32, (X_SLOPE, tile), 0)
    blk_t = lax.shift_right_logical(t0 + lax.broadcasted_iota(jnp.int32, (1, tile), 1), int(math.log2(SLC_LEN)))
    imp = [sum(_dot_nt(ovt_ref[...], t) for t in _split3(ps))[:X_SLOPE] for ps in p_sum]
    imp = [jnp.where(jj <= blk_t, x, NEG) for x in imp]
    imp = [jnp.where((jj == blk_t) | (jj == 0), FORCE, x) for x in imp]
    imp = [jnp.where(jj < n_sel, x, -3e38) for x in imp]
    rank = [jnp.zeros((X_SLOPE, tile), F32) for _ in groups]
    for j in range(n_sel):
        for g in groups:
            row = imp[g][j:j + 1, :]
            ge = jnp.where(row >= imp[g], 1.0, 0.0)
            gt = jnp.where(row > imp[g], 1.0, 0.0)
            rank[g] = rank[g] + jnp.where(jj > j, ge, gt)
    k_sel = float(min(SLC_TOPK, n_sel))
    bias_t = [jnp.where((rk < k_sel) | (jj >= n_sel), 0.0, NEG) for rk in rank]
    zero_rows = jnp.zeros((LANES - X_SLOPE, tile), F32)
    sel_bias = [jnp.concatenate([bt, zero_rows], axis=0).T[:, :d] for bt in bias_t]
    q_sel = [jnp.concatenate([q64[g], q_x[g] + jnp.concatenate([sel_bias[g]] * rep, axis=0)],
                             axis=1).astype(BF16) for g in groups]

    for g in groups:
        m_slc[g] = jnp.full((rows, LANES), -jnp.inf, F32)
        m_win[g] = jnp.full((rows, LANES), -jnp.inf, F32)
        acc_slc[g] = jnp.zeros((rows, 2 * LANES), F32)
        acc_win[g] = jnp.zeros((rows, 2 * LANES), F32)

    r_diag = pl.ds(pl.multiple_of(t0, tile), tile)
    attend([(q_sel[g], ksa[g, r_diag, :], vsa[g, r_diag, :], lower, m_slc, acc_slc, g) for g in groups])
    attend([(q_aug[g], kwa[g, r_diag, :], vwa[g, r_diag, :], lower, m_win, acc_win, g) for g in groups])

    @pl.when(it > 0)
    def _():
        r_prev = pl.ds(pl.multiple_of(t0 - tile, tile), tile)
        attend([(q_aug[g], kwa[g, r_prev, :], vwa[g, r_prev, :], upper, m_win, acc_win, g) for g in groups])

    picked = functools.reduce(jnp.maximum, [jnp.where(rk < k_sel, 1.0, 0.0) for rk in rank])
    blocks_per_tile = tile // SLC_LEN
    for kt in range(s_len // tile):
        tile_used[kt] = jnp.max(picked[kt * blocks_per_tile:(kt + 1) * blocks_per_tile, :]).astype(jnp.int32)

    def slc_tile(kt, carry):
        @pl.when(tile_used[kt] > 0)
        def _():
            r = pl.ds(pl.multiple_of(kt * tile, tile), tile)
            attend([(q_sel[g], ksa[g, r, :], vsa[g, r, :], None, m_slc, acc_slc, g) for g in groups])

        return carry

    lax.fori_loop(0, it, slc_tile, 0)

    g_hi, g_lo = _hi_lo(jax.nn.sigmoid(sm_ref[...]))
    gate_w = (jnp.dot(g_hi, gsel_ref[...], preferred_element_type=F32)
              + jnp.dot(g_lo, gsel_ref[...], preferred_element_type=F32))
    low_half = lax.broadcasted_iota(jnp.int32, (tile, LANES), 1) < d
    for g in groups:
        o_slc = acc_slc[g][:, :LANES] / acc_slc[g][:, LANES:]
        o_win = acc_win[g][:, :LANES] / acc_win[g][:, LANES:]
        for pr in range(rep // 2):
            pair = g * (rep // 2) + pr
            r_even = slice(2 * pr * tile, (2 * pr + 1) * tile)
            r_odd = slice((2 * pr + 1) * tile, (2 * pr + 2) * tile)
            merged = 0.0
            for br, o in enumerate((o_cmp[g], o_slc, o_win)):
                c0 = (pair * 3 + br) * LANES
                merged = merged + gate_w[:, c0:c0 + LANES] * jnp.where(low_half, o[r_even], o[r_odd])
            o_ref[:, pair * LANES:(pair + 1) * LANES] = merged.astype(BF16)


def _gate_spread():
    n_pairs = NSA_HEADS // 2
    c = np.arange(LANES)[:, None]
    col = np.arange(n_pairs * 3 * LANES)[None, :]
    pair, br, n = col // (3 * LANES), (col // LANES) % 3, col % LANES
    head = 2 * pair + (n >= NSA_HEAD_DIM)
    return jnp.asarray((c == SMALL_GATE + 3 * head + br).astype(np.float32), BF16)


def _head_sum(width=2 * LANES):
    i = np.arange(width)
    return jnp.asarray((i[:, None] // NSA_HEAD_DIM == i[None, :] // NSA_HEAD_DIM).astype(np.float32), BF16)


def _nsa(p, p_small, kc, vc, q_norm, kn_slc, kn_win, ovt, batch, seq):
    gsel, hsum = _gate_spread(), _head_sum()
    q_gain = jnp.tile(q_norm, (1, NSA_HEADS))
    tile = ATT_TILE
    nt = seq // tile
    rows = NSA_REP * tile
    kv = lambda col: pl.BlockSpec((seq, NSA_KV_W), lambda b, i, col=col: (b, col // 2),
                                  pipeline_mode=pl.Buffered(1))
    full = lambda a: pl.BlockSpec(a.shape, lambda b, i: (0,) * a.ndim)
    cmp_blk = pl.BlockSpec((1,) + kc.shape[1:], lambda b, i: (b, 0, 0))
    aug = lambda n: pltpu.VMEM((NSA_GROUPS, n, LANES), BF16)
    val = lambda n: pltpu.VMEM((NSA_GROUPS, n, 2 * LANES), BF16)
    stat = pltpu.VMEM((NSA_GROUPS, rows, LANES), F32)
    acc = pltpu.VMEM((NSA_GROUPS, rows, 2 * LANES), F32)
    return pl.pallas_call(
        _nsa_kernel,
        grid=(batch, nt),
        in_specs=[pl.BlockSpec((tile, NSA_W), lambda b, i: (b * nt + i, COL_NSQ // 8)),
                  kv(COL_KS), kv(COL_VS), kv(COL_KW), kv(COL_VW),
                  cmp_blk, cmp_blk,
                  pl.BlockSpec((tile, LANES), lambda b, i: (b * nt + i, 0)),
                  full(q_gain), full(kn_slc), full(kn_win), full(ovt), full(gsel), full(hsum)],
        out_specs=pl.BlockSpec((tile, NSA_W), lambda b, i: (b * nt + i, 0)),
        out_shape=jax.ShapeDtypeStruct((batch * seq, NSA_W), BF16),
        scratch_shapes=[aug(seq), val(seq), aug(seq), val(seq), aug(kc.shape[1]), val(kc.shape[1]),
                        stat, acc, stat, acc, pltpu.SMEM((nt,), jnp.int32)],
        compiler_params=_cparams(2),
        name="nsa_attention",
    )(p, p, p, p, p, kc, vc, p_small, q_gain, kn_slc, kn_win, ovt, gsel, hsum)


def _mix_kernel(x_ref, odn_ref, ons_ref, gdn_ref, gns_ref, wdn_ref, wns_ref, wout_ref, o_ref):
    y_dn = jnp.dot(odn_ref[...], wdn_ref[...], preferred_element_type=F32)
    y_ns = jnp.dot(ons_ref[...], wns_ref[...], preferred_element_type=F32)
    mix = (jax.nn.sigmoid(gdn_ref[...].astype(F32)) * y_dn
           + jax.nn.sigmoid(gns_ref[...].astype(F32)) * y_ns)
    o_ref[...] = x_ref[...] + jnp.dot(mix.astype(BF16), wout_ref[...], preferred_element_type=F32)


def _mix(x2, o_dn, o_ns, p, w_dn, w_ns, w_out, tm=512):
    t = x2.shape[0]
    row = lambda col: pl.BlockSpec((tm, D_MODEL), lambda i, col=col: (i, col))
    wfull = pl.BlockSpec((D_MODEL, D_MODEL), lambda i: (0, 0))
    return pl.pallas_call(
        _mix_kernel,
        grid=(t // tm,),
        in_specs=[row(0), row(0), row(0), row(COL_GDN // 8), row(COL_GNS // 8), wfull, wfull, wfull],
        out_specs=row(0),
        out_shape=jax.ShapeDtypeStruct((t, D_MODEL), F32),
        compiler_params=_cparams(1),
        name="mix_out",
    )(x2, o_dn, o_ns, p, p, w_dn, w_ns, w_out)


def _ffn_kernel(x_ref, g_ref, wup_ref, wdown_ref, o_ref, h_ref, acc_ref):
    f = pl.program_id(1)

    @pl.when(f == 0)
    def _():
        h_ref[...] = _rms(x_ref[...], g_ref[...]).astype(BF16)
        acc_ref[...] = jnp.zeros_like(acc_ref)

    u = jnp.dot(h_ref[...], wup_ref[...], preferred_element_type=F32)
    u = jnp.square(jnp.maximum(u, 0.0)).astype(BF16)
    acc_ref[...] += jnp.dot(u, wdown_ref[...], preferred_element_type=F32)

    @pl.when(f == pl.num_programs(1) - 1)
    def _():
        o_ref[...] = x_ref[...] + acc_ref[...]


def _ffn(x2, gain, w_up, w_down, tm=1024, tf=2048):
    t = x2.shape[0]
    return pl.pallas_call(
        _ffn_kernel,
        grid=(t // tm, D_FF // tf),
        in_specs=[pl.BlockSpec((tm, D_MODEL), lambda i, f: (i, 0)),
                  pl.BlockSpec((1, D_MODEL), lambda i, f: (0, 0)),
                  pl.BlockSpec((D_MODEL, tf), lambda i, f: (0, f)),
                  pl.BlockSpec((tf, D_MODEL), lambda i, f: (f, 0))],
        out_specs=pl.BlockSpec((tm, D_MODEL), lambda i, f: (i, 0)),
        out_shape=jax.ShapeDtypeStruct((t, D_MODEL), F32),
        scratch_shapes=[pltpu.VMEM((tm, D_MODEL), BF16), pltpu.VMEM((tm, D_MODEL), F32)],
        compiler_params=_cparams(2),
        name="ffn",
    )(x2, gain, w_up, w_down)


def _overlap_matrix_t(seq):
    n_cmp = (seq - CMP_LEN) // CMP_STRIDE + 1
    n_sel = seq // SLC_LEN
    c0 = np.arange(n_cmp)[None, :] * CMP_STRIDE
    j0 = np.arange(n_sel)[:, None] * SLC_LEN
    ov = np.clip(np.minimum(c0 + CMP_LEN, j0 + SLC_LEN) - np.maximum(c0, j0), 0, None) / CMP_LEN
    out = np.zeros((LANES, LANES), np.float32)
    out[:n_sel, :n_cmp] = ov
    return jnp.asarray(out, BF16)


def _regroup_w_in(w):
    o_b = 4 * DN_W
    o_q = o_b + 2 * DN_HEADS
    o_kv = o_q + NSA_W
    o_gate = o_kv + 6 * NSA_KV_W
    o_gdn = o_gate + 3 * NSA_HEADS
    pad = jnp.zeros((w.shape[0], P_WIDTH - w.shape[1]), w.dtype)
    return jnp.concatenate(
        [w[:, :o_b], w[:, o_q:o_kv], w[:, o_gdn:], w[:, o_kv:o_gate], w[:, o_b:o_q],
         w[:, o_gate:o_gdn], pad], axis=1).astype(BF16)


def _tokens16(p, col, batch, seq):
    t = p[:, col * LANES:col * LANES + NSA_KV_W]
    return t.reshape(batch, seq // CMP_STRIDE, CMP_STRIDE * NSA_KV_W)


def kernel(x, norm_mix, w_in, dn_conv, dn_a_log, dn_dt_bias, dn_out_norm, nsa_q_norm, nsa_k_norm_cmp,
           nsa_k_norm_slc, nsa_k_norm_win, cmp_pos_k, cmp_w1_k, cmp_w2_k, cmp_pos_v, cmp_w1_v, cmp_w2_v,
           w_proj_dn, w_proj_nsa, w_out, norm_mlp, w_up, w_down):
    batch, seq, _ = x.shape
    assert seq // CMP_STRIDE == LANES and seq % ATT_TILE == 0 and seq // SLC_LEN <= X_SLOPE
    ovt = _overlap_matrix_t(seq)
    x2 = x.reshape(batch * seq, D_MODEL)
    for l in range(w_in.shape[0]):
        p, p_small = _in_proj(x2, norm_mix[l][None], _regroup_w_in(w_in[l]))
        o_dn = _gdn(p, p_small, dn_conv[l], dn_a_log[l][None], dn_dt_bias[l][None], dn_out_norm[l][None],
                    batch, seq)
        kc, vc = _compress(_tokens16(p, COL_KC, batch, seq), _tokens16(p, COL_VC, batch, seq),
                           *_expand_compress_weights(cmp_pos_k[l], cmp_w1_k[l], cmp_w2_k[l]),
                           *_expand_compress_weights(cmp_pos_v[l], cmp_w1_v[l], cmp_w2_v[l]),
                           nsa_k_norm_cmp[l][None])
        o_ns = _nsa(p, p_small, kc, vc, nsa_q_norm[l][None], nsa_k_norm_slc[l][None], nsa_k_norm_win[l][None],
                    ovt, batch, seq)
        x2 = _mix(x2, o_dn, o_ns, p, w_proj_dn[l].astype(BF16), w_proj_nsa[l].astype(BF16),
                  w_out[l].astype(BF16))
        x2 = _ffn(x2, norm_mlp[l][None], w_up[l].astype(BF16), w_down[l].astype(BF16))
    return x2.reshape(batch, seq, D_MODEL)
```

```python
import functools
import math

import numpy as np
import jax
import jax.numpy as jnp
from jax import lax
from jax.experimental import pallas as pl
from jax.experimental.pallas import tpu as pltpu

F32 = jnp.float32
BF16 = jnp.bfloat16

D_MODEL = 1024
DN_HEADS = 8
DN_HEAD_DIM = 128
DN_W = DN_HEADS * DN_HEAD_DIM
DN_CONV = 4
NSA_HEADS = 16
NSA_GROUPS = 4
NSA_REP = NSA_HEADS // NSA_GROUPS
NSA_HEAD_DIM = 64
NSA_W = NSA_HEADS * NSA_HEAD_DIM
NSA_KV_W = NSA_GROUPS * NSA_HEAD_DIM
CMP_LEN = 32
CMP_STRIDE = 16
CMP_HIDDEN = 2 * NSA_HEAD_DIM
SLC_LEN = 64
SLC_TOPK = 8
WINDOW = 256
D_FF = 4 * D_MODEL
EPS = 1e-6
NEG = -1e30
FORCE = 1e9
LANES = 128

DN_CHUNK = LANES
DN_INV_BASE = 16
DN_INTRA_GROUP = 8
ATT_TILE = WINDOW
ATT_ROW_BLOCK = 128

COL_Q, COL_K, COL_V, COL_Z = 0, 8, 16, 24
COL_NSQ, COL_GDN, COL_GNS = 32, 40, 48
COL_KC, COL_VC, COL_KS, COL_VS, COL_KW, COL_VW = 56, 58, 60, 62, 64, 66
COL_SMALL = 68
P_WIDTH = 72 * LANES
SMALL_B, SMALL_A, SMALL_GATE = 0, DN_HEADS, 2 * DN_HEADS

X_SLOPE = SLC_TOPK * 4

VMEM_LIMIT = 56 * 1024 * 1024


def _cparams(n_axes):
    return pltpu.CompilerParams(dimension_semantics=("arbitrary",) * n_axes,
                                vmem_limit_bytes=VMEM_LIMIT)


def _bdot(a, b):
    return jnp.dot(a.astype(BF16), b.astype(BF16), preferred_element_type=F32)


def _dot_nt(a, b):
    return lax.dot_general(a.astype(BF16), b.astype(BF16), (((1,), (1,)), ((), ())),
                           preferred_element_type=F32)


def _dot_tn(a, b):
    return lax.dot_general(a.astype(BF16), b.astype(BF16), (((0,), (0,)), ((), ())),
                           preferred_element_type=F32)


def _split3(a):
    a1 = a.astype(BF16)
    r = a - a1.astype(F32)
    a2 = r.astype(BF16)
    a3 = (r - a2.astype(F32)).astype(BF16)
    return a1, a2, a3


def _hi_lo(a):
    hi = a.astype(BF16)
    return hi, (a - hi.astype(F32)).astype(BF16)


def _dot_hl(x, y):
    (xh, xl), (yh, yl) = x, y
    return (jnp.dot(jnp.concatenate([xh, xl], axis=1), jnp.concatenate([yh, yh], axis=0),
                    preferred_element_type=F32)
            + jnp.dot(xh, yl, preferred_element_type=F32))


def _dot_sel_rhs(a, sel):
    return sum(jnp.dot(t, sel, preferred_element_type=F32) for t in _split3(a))


def _dot_sel_lhs(sel, b):
    return sum(jnp.dot(sel, t, preferred_element_type=F32) for t in _split3(b))


def _rms(x, gain):
    return x * lax.rsqrt(jnp.mean(x * x, axis=-1, keepdims=True) + EPS) * gain


def _inproj_kernel(x_ref, g_ref, w_ref, ws_ref, o_ref, os_ref, h_ref):
    @pl.when(pl.program_id(1) == 0)
    def _():
        h_ref[...] = _rms(x_ref[...], g_ref[...]).astype(BF16)
        os_ref[...] = jnp.dot(h_ref[...], ws_ref[...], preferred_element_type=F32)

    o_ref[...] = jnp.dot(h_ref[...], w_ref[...], preferred_element_type=F32).astype(BF16)


def _in_proj(x2, gain, w_perm, tm=1024, tn=3072):
    t = x2.shape[0]
    return pl.pallas_call(
        _inproj_kernel,
        grid=(t // tm, P_WIDTH // tn),
        in_specs=[pl.BlockSpec((tm, D_MODEL), lambda i, j: (i, 0)),
                  pl.BlockSpec((1, D_MODEL), lambda i, j: (0, 0)),
                  pl.BlockSpec((D_MODEL, tn), lambda i, j: (0, j)),
                  pl.BlockSpec((D_MODEL, LANES), lambda i, j: (0, COL_SMALL))],
        out_specs=[pl.BlockSpec((tm, tn), lambda i, j: (i, j)),
                   pl.BlockSpec((tm, LANES), lambda i, j: (i, 0))],
        out_shape=[jax.ShapeDtypeStruct((t, P_WIDTH), BF16), jax.ShapeDtypeStruct((t, LANES), F32)],
        scratch_shapes=[pltpu.VMEM((tm, D_MODEL), BF16)],
        compiler_params=_cparams(2),
        name="in_proj",
    )(x2, gain, w_perm, w_perm)


def _conv_silu(x, w):
    rows = lax.broadcasted_iota(jnp.int32, x.shape, 0)
    y = x * w[DN_CONV - 1:DN_CONV, :]
    for j in range(DN_CONV - 1):
        sh = DN_CONV - 1 - j
        xs = jnp.where(rows >= sh, pltpu.roll(x, sh, axis=0), 0.0)
        y = y + xs * w[j:j + 1, :]
    return y * jax.nn.sigmoid(y)


def _gdn_kernel(alog_ref, dtb_ref, q_ref, k_ref, v_ref, z_ref, sm_ref, cq_ref, ck_ref, cv_ref,
                onorm_ref, o_ref, qs, ks, vs, gs, bs, us, ws, qks, os_, gls):
    h = pl.program_id(1)
    s_len = q_ref.shape[0]
    c = DN_CHUNK
    n_chunks = s_len // c

    q = _conv_silu(q_ref[...].astype(F32), cq_ref[...])
    k = _conv_silu(k_ref[...].astype(F32), ck_ref[...])
    v = _conv_silu(v_ref[...].astype(F32), cv_ref[...])
    qs[...] = q * lax.rsqrt(jnp.sum(q * q, axis=-1, keepdims=True) + EPS) * (DN_HEAD_DIM ** -0.5)
    ks[...] = k * lax.rsqrt(jnp.sum(k * k, axis=-1, keepdims=True) + EPS)
    vs[...] = v

    kk = lax.broadcasted_iota(jnp.int32, (LANES, 2 * LANES), 0)
    nn = lax.broadcasted_iota(jnp.int32, (LANES, 2 * LANES), 1)
    onehot = (kk == jnp.where(nn < LANES, SMALL_B + h, SMALL_A + h)).astype(BF16)
    ba = _dot_sel_rhs(sm_ref[...], onehot)
    neg_a = -jnp.exp(jnp.full((1, LANES), alog_ref[0, h], F32))
    dtb = jnp.full((1, LANES), dtb_ref[0, h], F32)
    bs[...] = jax.nn.sigmoid(ba[:, :LANES])
    gs[...] = neg_a * jax.nn.softplus(ba[:, LANES:] + dtb)

    ri = lax.broadcasted_iota(jnp.int32, (c, c), 0)
    ci = lax.broadcasted_iota(jnp.int32, (c, c), 1)
    causal = ri >= ci
    strict = ri > ci
    eye = (ri == ci).astype(F32)
    ltri = causal.astype(BF16)

    def same_blk(n):
        sh = int(math.log2(n))
        return lax.shift_right_logical(ri, sh) == lax.shift_right_logical(ci, sh)

    def intra(ig, carry):
        grp = range(DN_INTRA_GROUP)
        rs = [pl.ds((ig * DN_INTRA_GROUP + j) * c, c) for j in grp]
        qc, kc, vc, bc = ([ref[r, :] for r in rs] for ref in (qs, ks, vs, bs))
        gcum = [_dot_sel_lhs(ltri, gs[r, :]) for r in rs]
        decay = [jnp.exp(jnp.where(causal, g - g.T, -jnp.inf)) for g in gcum]
        kb = [kc[j] * bc[j] for j in grp]
        a = [jnp.where(strict, _dot_nt(kb[j], kc[j]) * decay[j], 0.0) for j in grp]
        tinv = [eye - jnp.where(same_blk(DN_INV_BASE), a[j], 0.0) for j in grp]
        pw = [_hi_lo(jnp.where(same_blk(DN_INV_BASE), a[j], 0.0)) for j in grp]
        for _ in range(int(math.log2(DN_INV_BASE)) - 1):
            pw = [_hi_lo(_dot_hl(x, x)) for x in pw]
            tinv = [tinv[j] + _dot_hl(_hi_lo(tinv[j]), pw[j]) for j in grp]
        n = DN_INV_BASE
        while n < c:
            low = [_hi_lo(jnp.where(same_blk(2 * n), jnp.where(same_blk(n), 0.0, a[j]), 0.0)) for j in grp]
            th = [_hi_lo(tinv[j]) for j in grp]
            lt = [_hi_lo(_dot_hl(low[j], th[j])) for j in grp]
            tinv = [tinv[j] - _dot_hl(th[j], lt[j]) for j in grp]
            n *= 2
        eg = [jnp.exp(g) for g in gcum]
        uw = [_bdot(tinv[j], jnp.concatenate([vc[j] * bc[j], kb[j] * eg[j]], axis=1)) for j in grp]
        qk = [_dot_nt(qc[j], kc[j]) * decay[j] for j in grp]
        glast = [g[c - 1:c, :] for g in gcum]
        k_end = [kc[j] * jnp.exp(glast[j] - gcum[j]) for j in grp]
        qk_uw = [_bdot(qk[j], uw[j]) for j in grp]
        ke_uw = [_dot_tn(k_end[j], uw[j]) for j in grp]
        for j in grp:
            r = rs[j]
            qs[r, :] = qc[j] * eg[j] - qk_uw[j][:, LANES:]
            qks[r, :] = qk_uw[j][:, :LANES]
            ws[r, :] = ke_uw[j][:, LANES:]
            us[r, :] = ke_uw[j][:, :LANES]
            gls[pl.ds(ig * DN_INTRA_GROUP + j, 1), :] = jnp.exp(glast[j])
        return carry

    for ig in range(n_chunks // DN_INTRA_GROUP):
        intra(ig, 0)

    state = jnp.zeros((DN_HEAD_DIM, DN_HEAD_DIM), F32)
    for ic in range(n_chunks):
        r = pl.ds(ic * c, c)
        os_[r, :] = _bdot(qs[r, :], state) + qks[r, :]
        state = state * gls[pl.ds(ic, 1), :] - _bdot(ws[r, :], state) + us[r, :]

    z = z_ref[...].astype(F32)
    o_ref[...] = (_rms(os_[...], onorm_ref[...]) * (z * jax.nn.sigmoid(z))).astype(BF16)


def _gdn(p, p_small, conv_w, a_log, dt_bias, out_norm, batch, seq):
    row_blk = lambda col0: pl.BlockSpec((seq, LANES), lambda b, h, col0=col0: (b, col0 + h))
    conv_blk = lambda col0: pl.BlockSpec((DN_CONV, LANES), lambda b, h, col0=col0: (0, col0 + h))
    smem = pl.BlockSpec(memory_space=pltpu.SMEM)
    return pl.pallas_call(
        _gdn_kernel,
        grid=(batch, DN_HEADS),
        in_specs=[smem, smem,
                  row_blk(COL_Q), row_blk(COL_K), row_blk(COL_V), row_blk(COL_Z),
                  pl.BlockSpec((seq, LANES), lambda b, h: (b, 0)),
                  conv_blk(COL_Q), conv_blk(COL_K), conv_blk(COL_V),
                  pl.BlockSpec((1, LANES), lambda b, h: (0, 0))],
        out_specs=pl.BlockSpec((seq, LANES), lambda b, h: (b, h)),
        out_shape=jax.ShapeDtypeStruct((batch * seq, DN_W), BF16),
        scratch_shapes=([pltpu.VMEM((seq, LANES), F32) for _ in range(9)]
                        + [pltpu.VMEM((seq // DN_CHUNK, LANES), F32)]),
        compiler_params=_cparams(2),
        name="gdn",
    )(a_log, dt_bias, p, p, p, p, p_small, conv_w, conv_w, conv_w, out_norm)


def _gelu_tanh(x):
    return 0.5 * x * (1.0 + jnp.tanh(math.sqrt(2.0 / math.pi) * (x + 0.044715 * (x * x * x))))


def _compress_one(t_ref, pos_ref, w1_ref, w2_ref):
    t = t_ref[0].astype(F32)
    n = t.shape[0]
    y1 = _bdot(t + pos_ref[0], w1_ref[0])
    y2 = _bdot(t + pos_ref[1], w1_ref[1])
    hid = _gelu_tanh(y1 + pltpu.roll(y2, n - 1, axis=0))
    return _bdot(hid, w2_ref[...])


def _compress_kernel(tk_ref, tv_ref, pk_ref, pv_ref, w1k_ref, w2k_ref, w1v_ref, w2v_ref, kn_ref,
                     kc_ref, vc_ref):
    d = NSA_HEAD_DIM
    kc = _compress_one(tk_ref, pk_ref, w1k_ref, w2k_ref)
    kc_ref[0] = jnp.concatenate([_rms(kc[:, g * d:(g + 1) * d], kn_ref[...]) for g in range(NSA_GROUPS)],
                                axis=1)
    vc_ref[0] = _compress_one(tv_ref, pv_ref, w1v_ref, w2v_ref)


def _expand_compress_weights(pos, w1, w2):
    g, d, h = NSA_GROUPS, NSA_HEAD_DIM, CMP_HIDDEN
    eye = jnp.eye(g, dtype=F32)
    w1e = jnp.einsum('aldh,gk->algdkh', w1.reshape(2, CMP_STRIDE, d, h), eye)
    w1e = w1e.reshape(2, CMP_STRIDE * g * d, g * h).astype(BF16)
    w2e = jnp.einsum('hd,gk->ghkd', w2, eye).reshape(g * h, g * d).astype(BF16)
    pose = jnp.broadcast_to(pos.reshape(2, CMP_STRIDE, 1, d), (2, CMP_STRIDE, g, d))
    return pose.reshape(2, 1, CMP_STRIDE * g * d), w1e, w2e


def _compress(tk, tv, pos_k, w1k, w2k, pos_v, w1v, w2v, kn_cmp):
    batch, n, width = tk.shape
    tok = pl.BlockSpec((1, n, width), lambda b: (b, 0, 0))
    full = lambda a: pl.BlockSpec(a.shape, lambda b: (0,) * a.ndim, pipeline_mode=pl.Buffered(1))
    out = pl.BlockSpec((1, n, NSA_KV_W), lambda b: (b, 0, 0))
    return pl.pallas_call(
        _compress_kernel,
        grid=(batch,),
        in_specs=[tok, tok, full(pos_k), full(pos_v), full(w1k), full(w2k), full(w1v), full(w2v),
                  full(kn_cmp)],
        out_specs=[out, out],
        out_shape=[jax.ShapeDtypeStruct((batch, n, NSA_KV_W), F32)] * 2,
        compiler_params=_cparams(1),
        name="nsa_compress",
    )(tk, tv, pos_k, pos_v, w1k, w2k, w1v, w2v, kn_cmp)


def _nsa_kernel(q_ref, ks_ref, vs_ref, kw_ref, vw_ref, kc_ref, vc_ref, sm_ref, qn_ref, kns_ref,
                knw_ref, ovt_ref, gsel_ref, hsum_ref, o_ref, ksa, vsa, kwa, vwa, kca, vca, m_slc, acc_slc,
                m_win, acc_win, tile_used):
    it = pl.program_id(1)
    tile = ATT_TILE
    s_len = ks_ref.shape[0]
    d = NSA_HEAD_DIM
    rep = NSA_REP
    rows = rep * tile
    n_sel = s_len // SLC_LEN
    n_cend = kc_ref.shape[1]

    @pl.when(it == 0)
    def _():
        rowi = lax.broadcasted_iota(jnp.int32, (s_len, d), 0)
        xl = lax.broadcasted_iota(jnp.int32, (s_len, d), 1)
        in_tile = (rowi & (tile - 1)).astype(F32)
        tile_start = (rowi - (rowi & (tile - 1))).astype(F32)
        pos_cols = jnp.where((xl == X_SLOPE) | (xl == X_SLOPE + 1), in_tile,
                             jnp.where((xl == X_SLOPE + 2) | (xl == X_SLOPE + 3), tile_start, 0.0))
        blk_cols = jnp.where(xl == lax.shift_right_logical(rowi, int(math.log2(SLC_LEN))), 1.0, 0.0)
        ci = lax.broadcasted_iota(jnp.int32, (n_cend, d), 0)
        cl = lax.broadcasted_iota(jnp.int32, (n_cend, d), 1)
        cend_cols = jnp.where((cl == X_SLOPE) | (cl == X_SLOPE + 1), (CMP_STRIDE * ci).astype(F32), 0.0)
        ones_s = jnp.ones((s_len, LANES), F32)
        ones_c = jnp.ones((n_cend, LANES), F32)
        for g in range(NSA_GROUPS):
            sl = slice(g * d, (g + 1) * d)
            ksa[g] = jnp.concatenate([_rms(ks_ref[:, sl].astype(F32), kns_ref[...]), pos_cols + blk_cols],
                                     axis=1).astype(BF16)
            kwa[g] = jnp.concatenate([_rms(kw_ref[:, sl].astype(F32), knw_ref[...]), pos_cols],
                                     axis=1).astype(BF16)
            kca[g] = jnp.concatenate([kc_ref[0, :, sl], cend_cols], axis=1).astype(BF16)
            vs_g, vw_g = vs_ref[:, sl].astype(F32), vw_ref[:, sl].astype(F32)
            vsa[g] = jnp.concatenate([vs_g, vs_g, ones_s], axis=1).astype(BF16)
            vwa[g] = jnp.concatenate([vw_g, vw_g, ones_s], axis=1).astype(BF16)
            vca[g] = jnp.concatenate([vc_ref[0, :, sl], vc_ref[0, :, sl], ones_c], axis=1).astype(BF16)

    t0 = it * tile
    groups = range(NSA_GROUPS)
    row_r =lax.shift_right_logical(lax.broadcasted_iota(jnp.int32, (rows, d), 0),
                                    int(math.log2(tile)))
    tpos = t0 + (lax.broadcasted_iota(jnp.int32, (rows, LANES), 0) & (tile - 1))
    xl = lax.broadcasted_iota(jnp.int32, (rows, d), 1)
    col = lax.broadcasted_iota(jnp.int32, (rows, tile), 1)
    row_tt = lax.broadcasted_iota(jnp.int32, (rows, tile), 0) & (tile - 1)
    lower = col <= row_tt
    upper = col > row_tt
    cend = CMP_STRIDE * lax.broadcasted_iota(jnp.int32, (1, LANES), 1) + (CMP_LEN - 1)
    cmp_ok = tpos >= cend
    any_valid = (tpos >= CMP_LEN - 1).astype(F32)

    def attend(items):
        def scores(x, rb):
            r = slice(rb * ATT_ROW_BLOCK, (rb + 1) * ATT_ROW_BLOCK)
            s = _dot_nt(x[0][r], x[1])
            if x[3] is not None:
                s = jnp.where(x[3][r], s, NEG)
            return s, jnp.max(s, axis=-1, keepdims=True)

        def update(x, rb, s, s_max):
            r = slice(rb * ATT_ROW_BLOCK, (rb + 1) * ATT_ROW_BLOCK)
            g = x[6]
            m_old = x[4][g, r, :]
            m_new = jnp.maximum(m_old, jnp.broadcast_to(s_max, m_old.shape))
            alpha = jnp.exp2(m_old - m_new)
            p = jnp.exp2(s - jnp.concatenate([m_new] * (tile // LANES), axis=1))
            x[5][g, r, :] = jnp.concatenate([alpha, alpha], axis=1) * x[5][g, r, :] + _bdot(p, x[2])
            x[4][g, r, :] = m_new

        pending = None
        for x in items:
            for rb in range(rows // ATT_ROW_BLOCK):
                cur = (x, rb) + scores(x, rb)
                if pending is not None:
                    update(*pending)
                pending = cur
        update(*pending)

    log2e = math.log2(math.e)
    qf = q_ref[...].astype(F32)
    hw = hsum_ref.shape[0]
    sq_hi, sq_lo = _hi_lo(qf * qf)
    ssq = jnp.concatenate(
        [jnp.dot(sq_hi[:, j:j + hw], hsum_ref[...], preferred_element_type=F32)
         + jnp.dot(sq_lo[:, j:j + hw], hsum_ref[...], preferred_element_type=F32)
         for j in range(0, NSA_W, hw)], axis=1)
    qn = qf * lax.rsqrt(ssq * (1.0 / d) + EPS) * (qn_ref[...] * (log2e * d ** -0.5))
    q64, q_x, q_aug = [], [], []
    for g in groups:
        slopes = [log2e * 2.0 ** (-8.0 * (g * rep + r + 1) / NSA_HEADS) for r in range(rep)]
        sl = jnp.full((rows, d), slopes[rep - 1], F32)
        for r in range(rep - 2, -1, -1):
            sl = jnp.where(row_r == r, slopes[r], sl)
        sl_hi = sl.astype(BF16).astype(F32)
        q_x.append(jnp.where((xl == X_SLOPE) | (xl == X_SLOPE + 2), sl_hi,
                             jnp.where((xl == X_SLOPE + 1) | (xl == X_SLOPE + 3), sl - sl_hi, 0.0)))
        q64.append(jnp.concatenate(
            [qn[:, (g * rep + r) * d:(g * rep + r + 1) * d] for r in range(rep)], axis=0))
        q_aug.append(jnp.concatenate([q64[g], q_x[g]], axis=1).astype(BF16))

    s_c = [jnp.where(cmp_ok, _dot_nt(q_aug[g], kca[g]), NEG) for g in groups]
    e_c = [jnp.exp2(s - jnp.max(s, axis=-1, keepdims=True)) for s in s_c]
    pv_c = [_bdot(e_c[g], vca[g]) for g in groups]
    w_c = [any_valid / pv[:, LANES:] for pv in pv_c]
    p_c = [e_c[g] * w_c[g] for g in groups]
    o_cmp = [pv_c[g][:, :LANES] * w_c[g] for g in groups]
    p_sum = [sum(p[r * tile:(r + 1) * tile] for r in range(rep)) for p in p_c]

    jj = lax.broadcasted_iota(jnp.int32, (X_SLOPE, tile), 0)
    blk_t = lax.shift_right_logical(t0 + lax.broadcasted_iota(jnp.int32, (1, tile), 1), int(math.log2(SLC_LEN)))
    imp = [sum(_dot_nt(ovt_ref[...], t) for t in _split3(ps))[:X_SLOPE] for ps in p_sum]
    imp = [jnp.where(jj <= blk_t, x, NEG) for x in imp]
    imp = [jnp.where((jj == blk_t) | (jj == 0), FORCE, x) for x in imp]
    imp = [jnp.where(jj < n_sel, x, -3e38) for x in imp]
    rank = [jnp.zeros((X_SLOPE, tile), F32) for _ in groups]
    for j in range(n_sel):
        for g in groups:
            row = imp[g][j:j + 1, :]
            ge = jnp.where(row >= imp[g], 1.0, 0.0)
            gt = jnp.where(row > imp[g], 1.0, 0.0)
            rank[g] = rank[g] + jnp.where(jj > j, ge, gt)
    k_sel = float(min(SLC_TOPK, n_sel))
    bias_t = [jnp.where((rk < k_sel) | (jj >= n_sel), 0.0, NEG) for rk in rank]
    zero_rows = jnp.zeros((LANES - X_SLOPE, tile), F32)
    sel_bias = [jnp.concatenate([bt, zero_rows], axis=0).T[:, :d] for bt in bias_t]
    q_sel = [jnp.concatenate([q64[g], q_x[g] + jnp.concatenate([sel_bias[g]] * rep, axis=0)],
                             axis=1).astype(BF16) for g in groups]

    for g in groups:
        m_slc[g] = jnp.full((rows, LANES), -jnp.inf, F32)
        m_win[g] = jnp.full((rows, LANES), -jnp.inf, F32)
        acc_slc[g] = jnp.zeros((rows, 2 * LANES), F32)
        acc_win[g] = jnp.zeros((rows, 2 * LANES), F32)

    r_diag = pl.ds(pl.multiple_of(t0, tile), tile)
    attend([(q_sel[g], ksa[g, r_diag, :], vsa[g, r_diag, :], lower, m_slc, acc_slc, g) for g in groups])
    attend([(q_aug[g], kwa[g, r_diag, :], vwa[g, r_diag, :], lower, m_win, acc_win, g) for g in groups])

    @pl.when(it > 0)
    def _():
        r_prev = pl.ds(pl.multiple_of(t0 - tile, tile), tile)
        attend([(q_aug[g], kwa[g, r_prev, :], vwa[g, r_prev, :], upper, m_win, acc_win, g) for g in groups])

    picked = functools.reduce(jnp.maximum, [jnp.where(rk < k_sel, 1.0, 0.0) for rk in rank])
    blocks_per_tile = tile // SLC_LEN
    for kt in range(s_len // tile):
        tile_used[kt] = jnp.max(picked[kt * blocks_per_tile:(kt + 1) * blocks_per_tile, :]).astype(jnp.int32)

    def slc_tile(kt, carry):
        @pl.when(tile_used[kt] > 0)
        def _():
            r = pl.ds(pl.multiple_of(kt * tile, tile), tile)
            attend([(q_sel[g], ksa[g, r, :], vsa[g, r, :], None, m_slc, acc_slc, g) for g in groups])

        return carry

    lax.fori_loop(0, it, slc_tile, 0)

    g_hi, g_lo = _hi_lo(jax.nn.sigmoid(sm_ref[...]))
    gate_w = (jnp.dot(g_hi, gsel_ref[...], preferred_element_type=F32)
              + jnp.dot(g_lo, gsel_ref[...], preferred_element_type=F32))
    low_half = lax.broadcasted_iota(jnp.int32, (tile, LANES), 1) < d
    for g in groups:
        o_slc = acc_slc[g][:, :LANES] / acc_slc[g][:, LANES:]
        o_win = acc_win[g][:, :LANES] / acc_win[g][:, LANES:]
        for pr in range(rep // 2):
            pair = g * (rep // 2) + pr
            r_even = slice(2 * pr * tile, (2 * pr + 1) * tile)
            r_odd = slice((2 * pr + 1) * tile, (2 * pr + 2) * tile)
            merged = 0.0
            for br, o in enumerate((o_cmp[g], o_slc, o_win)):
                c0 = (pair * 3 + br) * LANES
                merged = merged + gate_w[:, c0:c0 + LANES] * jnp.where(low_half, o[r_even], o[r_odd])
            o_ref[:, pair * LANES:(pair + 1) * LANES] = merged.astype(BF16)


def _gate_spread():
    n_pairs = NSA_HEADS // 2
    c = np.arange(LANES)[:, None]
    col = np.arange(n_pairs * 3 * LANES)[None, :]
    pair, br, n = col // (3 * LANES), (col // LANES) % 3, col % LANES
    head = 2 * pair + (n >= NSA_HEAD_DIM)
    return jnp.asarray((c == SMALL_GATE + 3 * head + br).astype(np.float32), BF16)


def _head_sum(width=2 * LANES):
    i = np.arange(width)
    return jnp.asarray((i[:, None] // NSA_HEAD_DIM == i[None, :] // NSA_HEAD_DIM).astype(np.float32), BF16)


def _nsa(p, p_small, kc, vc, q_norm, kn_slc, kn_win, ovt, batch, seq):
    gsel, hsum = _gate_spread(), _head_sum()
    q_gain = jnp.tile(q_norm, (1, NSA_HEADS))
    tile = ATT_TILE
    nt = seq // tile
    rows = NSA_REP * tile
    kv = lambda col: pl.BlockSpec((seq, NSA_KV_W), lambda b, i, col=col: (b, col // 2),
                                  pipeline_mode=pl.Buffered(1))
    full = lambda a: pl.BlockSpec(a.shape, lambda b, i: (0,) * a.ndim)
    cmp_blk = pl.BlockSpec((1,) + kc.shape[1:], lambda b, i: (b, 0, 0))
    aug = lambda n: pltpu.VMEM((NSA_GROUPS, n, LANES), BF16)
    val = lambda n: pltpu.VMEM((NSA_GROUPS, n, 2 * LANES), BF16)
    stat = pltpu.VMEM((NSA_GROUPS, rows, LANES), F32)
    acc = pltpu.VMEM((NSA_GROUPS, rows, 2 * LANES), F32)
    return pl.pallas_call(
        _nsa_kernel,
        grid=(batch, nt),
        in_specs=[pl.BlockSpec((tile, NSA_W), lambda b, i: (b * nt + i, COL_NSQ // 8)),
                  kv(COL_KS), kv(COL_VS), kv(COL_KW), kv(COL_VW),
                  cmp_blk, cmp_blk,
                  pl.BlockSpec((tile, LANES), lambda b, i: (b * nt + i, 0)),
                  full(q_gain), full(kn_slc), full(kn_win), full(ovt), full(gsel), full(hsum)],
        out_specs=pl.BlockSpec((tile, NSA_W), lambda b, i: (b * nt + i, 0)),
        out_shape=jax.ShapeDtypeStruct((batch * seq, NSA_W), BF16),
        scratch_shapes=[aug(seq), val(seq), aug(seq), val(seq), aug(kc.shape[1]), val(kc.shape[1]),
                        stat, acc, stat, acc, pltpu.SMEM((nt,), jnp.int32)],
        compiler_params=_cparams(2),
        name="nsa_attention",
    )(p, p, p, p, p, kc, vc, p_small, q_gain, kn_slc, kn_win, ovt, gsel, hsum)


def _mix_kernel(x_ref, odn_ref, ons_ref, gdn_ref, gns_ref, wdn_ref, wns_ref, wout_ref, o_ref):
    y_dn = jnp.dot(odn_ref[...], wdn_ref[...], preferred_element_type=F32)
    y_ns = jnp.dot(ons_ref[...], wns_ref[...], preferred_element_type=F32)
    mix = (jax.nn.sigmoid(gdn_ref[...].astype(F32)) * y_dn
           + jax.nn.sigmoid(gns_ref[...].astype(F32)) * y_ns)
    o_ref[...] = x_ref[...] + jnp.dot(mix.astype(BF16), wout_ref[...], preferred_element_type=F32)


def _mix(x2, o_dn, o_ns, p, w_dn, w_ns, w_out, tm=512):
    t = x2.shape[0]
    row = lambda col: pl.BlockSpec((tm, D_MODEL), lambda i, col=col: (i, col))
    wfull = pl.BlockSpec((D_MODEL, D_MODEL), lambda i: (0, 0))
    return pl.pallas_call(
        _mix_kernel,
        grid=(t // tm,),
        in_specs=[row(0), row(0), row(0), row(COL_GDN // 8), row(COL_GNS // 8), wfull, wfull, wfull],
        out_specs=row(0),
        out_shape=jax.ShapeDtypeStruct((t, D_MODEL), F32),
        compiler_params=_cparams(1),
        name="mix_out",
    )(x2, o_dn, o_ns, p, p, w_dn, w_ns, w_out)


def _ffn_kernel(x_ref, g_ref, wup_ref, wdown_ref, o_ref, h_ref, acc_ref):
    f = pl.program_id(1)

    @pl.when(f == 0)
    def _():
        h_ref[...] = _rms(x_ref[...], g_ref[...]).astype(BF16)
        acc_ref[...] = jnp.zeros_like(acc_ref)

    u = jnp.dot(h_ref[...], wup_ref[...], preferred_element_type=F32)
    u = jnp.square(jnp.maximum(u, 0.0)).astype(BF16)
    acc_ref[...] += jnp.dot(u, wdown_ref[...], preferred_element_type=F32)

    @pl.when(f == pl.num_programs(1) - 1)
    def _():
        o_ref[...] = x_ref[...] + acc_ref[...]


def _ffn(x2, gain, w_up, w_down, tm=1024, tf=2048):
    t = x2.shape[0]
    return pl.pallas_call(
        _ffn_kernel,
        grid=(t // tm, D_FF // tf),
        in_specs=[pl.BlockSpec((tm, D_MODEL), lambda i, f: (i, 0)),
                  pl.BlockSpec((1, D_MODEL), lambda i, f: (0, 0)),
                  pl.BlockSpec((D_MODEL, tf), lambda i, f: (0, f)),
                  pl.BlockSpec((tf, D_MODEL), lambda i, f: (f, 0))],
        out_specs=pl.BlockSpec((tm, D_MODEL), lambda i, f: (i, 0)),
        out_shape=jax.ShapeDtypeStruct((t, D_MODEL), F32),
        scratch_shapes=[pltpu.VMEM((tm, D_MODEL), BF16), pltpu.VMEM((tm, D_MODEL), F32)],
        compiler_params=_cparams(2),
        name="ffn",
    )(x2, gain, w_up, w_down)


def _overlap_matrix_t(seq):
    n_cmp = (seq - CMP_LEN) // CMP_STRIDE + 1
    n_sel = seq // SLC_LEN
    c0 = np.arange(n_cmp)[None, :] * CMP_STRIDE
    j0 = np.arange(n_sel)[:, None] * SLC_LEN
    ov = np.clip(np.minimum(c0 + CMP_LEN, j0 + SLC_LEN) - np.maximum(c0, j0), 0, None) / CMP_LEN
    out = np.zeros((LANES, LANES), np.float32)
    out[:n_sel, :n_cmp] = ov
    return jnp.asarray(out, BF16)


def _regroup_w_in(w):
    o_b = 4 * DN_W
    o_q = o_b + 2 * DN_HEADS
    o_kv = o_q + NSA_W
    o_gate = o_kv + 6 * NSA_KV_W
    o_gdn = o_gate + 3 * NSA_HEADS
    pad = jnp.zeros((w.shape[0], P_WIDTH - w.shape[1]), w.dtype)
    return jnp.concatenate(
        [w[:, :o_b], w[:, o_q:o_kv], w[:, o_gdn:], w[:, o_kv:o_gate], w[:, o_b:o_q],
         w[:, o_gate:o_gdn], pad], axis=1).astype(BF16)


def _tokens16(p, col, batch, seq):
    t = p[:, col * LANES:col * LANES + NSA_KV_W]
    return t.reshape(batch, seq // CMP_STRIDE, CMP_STRIDE * NSA_KV_W)


def kernel(x, norm_mix, w_in, dn_conv, dn_a_log, dn_dt_bias, dn_out_norm, nsa_q_norm, nsa_k_norm_cmp,
           nsa_k_norm_slc, nsa_k_norm_win, cmp_pos_k, cmp_w1_k, cmp_w2_k, cmp_pos_v, cmp_w1_v, cmp_w2_v,
           w_proj_dn, w_proj_nsa, w_out, norm_mlp, w_up, w_down):
    batch, seq, _ = x.shape
    assert seq // CMP_STRIDE == LANES and seq % ATT_TILE == 0 and seq // SLC_LEN <= X_SLOPE
    ovt = _overlap_matrix_t(seq)
    x2 = x.reshape(batch * seq, D_MODEL)
    for l in range(w_in.shape[0]):
        p, p_small = _in_proj(x2, norm_mix[l][None], _regroup_w_in(w_in[l]))
        o_dn = _gdn(p, p_small, dn_conv[l], dn_a_log[l][None], dn_dt_bias[l][None], dn_out_norm[l][None],
                    batch, seq)
        kc, vc = _compress(_tokens16(p, COL_KC, batch, seq), _tokens16(p, COL_VC, batch, seq),
                           *_expand_compress_weights(cmp_pos_k[l], cmp_w1_k[l], cmp_w2_k[l]),
                           *_expand_compress_weights(cmp_pos_v[l], cmp_w1_v[l], cmp_w2_v[l]),
                           nsa_k_norm_cmp[l][None])
        o_ns = _nsa(p, p_small, kc, vc, nsa_q_norm[l][None], nsa_k_norm_slc[l][None], nsa_k_norm_win[l][None],
                    ovt, batch, seq)
        x2 = _mix(x2, o_dn, o_ns, p, w_proj_dn[l].astype(BF16), w_proj_nsa[l].astype(BF16),
                  w_out[l].astype(BF16))
        x2 = _ffn(x2, norm_mlp[l][None], w_up[l].astype(BF16), w_down[l].astype(BF16))
    return x2.reshape(batch, seq, D_MODEL)
```

```python
import functools
import math

import numpy as np
import jax
import jax.numpy as jnp
from jax import lax
from jax.experimental import pallas as pl
from jax.experimental.pallas import tpu as pltpu

F32 = jnp.float32
BF16 = jnp.bfloat16

D_MODEL = 1024
DN_HEADS = 8
DN_HEAD_DIM = 128
DN_W = DN_HEADS * DN_HEAD_DIM
DN_CONV = 4
NSA_HEADS = 16
NSA_GROUPS = 4
NSA_REP = NSA_HEADS // NSA_GROUPS
NSA_HEAD_DIM = 64
NSA_W = NSA_HEADS * NSA_HEAD_DIM
NSA_KV_W = NSA_GROUPS * NSA_HEAD_DIM
CMP_LEN = 32
CMP_STRIDE = 16
CMP_HIDDEN = 2 * NSA_HEAD_DIM
SLC_LEN = 64
SLC_TOPK = 8
WINDOW = 256
D_FF = 4 * D_MODEL
EPS = 1e-6
NEG = -1e30
FORCE = 1e9
LANES = 128

DN_CHUNK = LANES
DN_INV_BASE = 16
DN_INTRA_GROUP = 8
ATT_TILE = WINDOW
ATT_ROW_BLOCK = 128

COL_Q, COL_K, COL_V, COL_Z = 0, 8, 16, 24
COL_NSQ, COL_GDN, COL_GNS = 32, 40, 48
COL_KC, COL_VC, COL_KS, COL_VS, COL_KW, COL_VW = 56, 58, 60, 62, 64, 66
COL_SMALL = 68
P_WIDTH = 72 * LANES
SMALL_B, SMALL_A, SMALL_GATE = 0, DN_HEADS, 2 * DN_HEADS

X_SLOPE = SLC_TOPK * 4

VMEM_LIMIT = 56 * 1024 * 1024


def _cparams(n_axes):
    return pltpu.CompilerParams(dimension_semantics=("arbitrary",) * n_axes,
                                vmem_limit_bytes=VMEM_LIMIT)


def _bdot(a, b):
    return jnp.dot(a.astype(BF16), b.astype(BF16), preferred_element_type=F32)


def _dot_nt(a, b):
    return lax.dot_general(a.astype(BF16), b.astype(BF16), (((1,), (1,)), ((), ())),
                           preferred_element_type=F32)


def _dot_tn(a, b):
    return lax.dot_general(a.astype(BF16), b.astype(BF16), (((0,), (0,)), ((), ())),
                           preferred_element_type=F32)


def _split3(a):
    a1 = a.astype(BF16)
    r = a - a1.astype(F32)
    a2 = r.astype(BF16)
    a3 = (r - a2.astype(F32)).astype(BF16)
    return a1, a2, a3


def _hi_lo(a):
    hi = a.astype(BF16)
    return hi, (a - hi.astype(F32)).astype(BF16)


def _dot_hl(x, y):
    (xh, xl), (yh, yl) = x, y
    return (jnp.dot(jnp.concatenate([xh, xl], axis=1), jnp.concatenate([yh, yh], axis=0),
                    preferred_element_type=F32)
            + jnp.dot(xh, yl, preferred_element_type=F32))


def _dot_sel_rhs(a, sel):
    return sum(jnp.dot(t, sel, preferred_element_type=F32) for t in _split3(a))


def _dot_sel_lhs(sel, b):
    return sum(jnp.dot(sel, t, preferred_element_type=F32) for t in _split3(b))


def _rms(x, gain):
    return x * lax.rsqrt(jnp.mean(x * x, axis=-1, keepdims=True) + EPS) * gain


def _inproj_kernel(x_ref, g_ref, w_ref, ws_ref, o_ref, os_ref, h_ref):
    @pl.when(pl.program_id(1) == 0)
    def _():
        h_ref[...] = _rms(x_ref[...], g_ref[...]).astype(BF16)
        os_ref[...] = jnp.dot(h_ref[...], ws_ref[...], preferred_element_type=F32)

    o_ref[...] = jnp.dot(h_ref[...], w_ref[...], preferred_element_type=F32).astype(BF16)


def _in_proj(x2, gain, w_perm, tm=1024, tn=3072):
    t = x2.shape[0]
    return pl.pallas_call(
        _inproj_kernel,
        grid=(t // tm, P_WIDTH // tn),
        in_specs=[pl.BlockSpec((tm, D_MODEL), lambda i, j: (i, 0)),
                  pl.BlockSpec((1, D_MODEL), lambda i, j: (0, 0)),
                  pl.BlockSpec((D_MODEL, tn), lambda i, j: (0, j)),
                  pl.BlockSpec((D_MODEL, LANES), lambda i, j: (0, COL_SMALL))],
        out_specs=[pl.BlockSpec((tm, tn), lambda i, j: (i, j)),
                   pl.BlockSpec((tm, LANES), lambda i, j: (i, 0))],
        out_shape=[jax.ShapeDtypeStruct((t, P_WIDTH), BF16), jax.ShapeDtypeStruct((t, LANES), F32)],
        scratch_shapes=[pltpu.VMEM((tm, D_MODEL), BF16)],
        compiler_params=_cparams(2),
        name="in_proj",
    )(x2, gain, w_perm, w_perm)


def _conv_silu(x, w):
    rows = lax.broadcasted_iota(jnp.int32, x.shape, 0)
    y = x * w[DN_CONV - 1:DN_CONV, :]
    for j in range(DN_CONV - 1):
        sh = DN_CONV - 1 - j
        xs = jnp.where(rows >= sh, pltpu.roll(x, sh, axis=0), 0.0)
        y = y + xs * w[j:j + 1, :]
    return y * jax.nn.sigmoid(y)


def _gdn_kernel(alog_ref, dtb_ref, q_ref, k_ref, v_ref, z_ref, sm_ref, cq_ref, ck_ref, cv_ref,
                onorm_ref, o_ref, qs, ks, vs, gs, bs, us, ws, qks, os_, gls):
    h = pl.program_id(1)
    s_len = q_ref.shape[0]
    c = DN_CHUNK
    n_chunks = s_len // c

    q = _conv_silu(q_ref[...].astype(F32), cq_ref[...])
    k = _conv_silu(k_ref[...].astype(F32), ck_ref[...])
    v = _conv_silu(v_ref[...].astype(F32), cv_ref[...])
    qs[...] = q * lax.rsqrt(jnp.sum(q * q, axis=-1, keepdims=True) + EPS) * (DN_HEAD_DIM ** -0.5)
    ks[...] = k * lax.rsqrt(jnp.sum(k * k, axis=-1, keepdims=True) + EPS)
    vs[...] = v

    kk = lax.broadcasted_iota(jnp.int32, (LANES, 2 * LANES), 0)
    nn = lax.broadcasted_iota(jnp.int32, (LANES, 2 * LANES), 1)
    onehot = (kk == jnp.where(nn < LANES, SMALL_B + h, SMALL_A + h)).astype(BF16)
    ba = _dot_sel_rhs(sm_ref[...], onehot)
    neg_a = -jnp.exp(jnp.full((1, LANES), alog_ref[0, h], F32))
    dtb = jnp.full((1, LANES), dtb_ref[0, h], F32)
    bs[...] = jax.nn.sigmoid(ba[:, :LANES])
    gs[...] = neg_a * jax.nn.softplus(ba[:, LANES:] + dtb)

    ri = lax.broadcasted_iota(jnp.int32, (c, c), 0)
    ci = lax.broadcasted_iota(jnp.int32, (c, c), 1)
    causal = ri >= ci
    strict = ri > ci
    eye = (ri == ci).astype(F32)
    ltri = causal.astype(BF16)

    def same_blk(n):
        sh = int(math.log2(n))
        return lax.shift_right_logical(ri, sh) == lax.shift_right_logical(ci, sh)

    def intra(ig, carry):
        grp = range(DN_INTRA_GROUP)
        rs = [pl.ds((ig * DN_INTRA_GROUP + j) * c, c) for j in grp]
        qc, kc, vc, bc = ([ref[r, :] for r in rs] for ref in (qs, ks, vs, bs))
        gcum = [_dot_sel_lhs(ltri, gs[r, :]) for r in rs]
        decay = [jnp.exp(jnp.where(causal, g - g.T, -jnp.inf)) for g in gcum]
        kb = [kc[j] * bc[j] for j in grp]
        a = [jnp.where(strict, _dot_nt(kb[j], kc[j]) * decay[j], 0.0) for j in grp]
        tinv = [eye - jnp.where(same_blk(DN_INV_BASE), a[j], 0.0) for j in grp]
        pw = [_hi_lo(jnp.where(same_blk(DN_INV_BASE), a[j], 0.0)) for j in grp]
        for _ in range(int(math.log2(DN_INV_BASE)) - 1):
            pw = [_hi_lo(_dot_hl(x, x)) for x in pw]
            tinv = [tinv[j] + _dot_hl(_hi_lo(tinv[j]), pw[j]) for j in grp]
        n = DN_INV_BASE
        while n < c:
            low = [jnp.where(same_blk(2 * n), jnp.where(same_blk(n), 0.0, a[j]), 0.0) for j in grp]
            lt = [_bdot(low[j], tinv[j]) for j in grp]
            tinv = [tinv[j] - _bdot(tinv[j], lt[j]) for j in grp]
            n *= 2
        eg = [jnp.exp(g) for g in gcum]
        uw = [_bdot(tinv[j], jnp.concatenate([vc[j] * bc[j], kb[j] * eg[j]], axis=1)) for j in grp]
        qk = [_dot_nt(qc[j], kc[j]) * decay[j] for j in grp]
        glast = [g[c - 1:c, :] for g in gcum]
        k_end = [kc[j] * jnp.exp(glast[j] - gcum[j]) for j in grp]
        qk_uw = [_bdot(qk[j], uw[j]) for j in grp]
        ke_uw = [_dot_tn(k_end[j], uw[j]) for j in grp]
        for j in grp:
            r = rs[j]
            qs[r, :] = qc[j] * eg[j] - qk_uw[j][:, LANES:]
            qks[r, :] = qk_uw[j][:, :LANES]
            ws[r, :] = ke_uw[j][:, LANES:]
            us[r, :] = ke_uw[j][:, :LANES]
            gls[pl.ds(ig * DN_INTRA_GROUP + j, 1), :] = jnp.exp(glast[j])
        return carry

    for ig in range(n_chunks // DN_INTRA_GROUP):
        intra(ig, 0)

    state = jnp.zeros((DN_HEAD_DIM, DN_HEAD_DIM), F32)
    for ic in range(n_chunks):
        r = pl.ds(ic * c, c)
        os_[r, :] = _bdot(qs[r, :], state) + qks[r, :]
        state = state * gls[pl.ds(ic, 1), :] - _bdot(ws[r, :], state) + us[r, :]

    z = z_ref[...].astype(F32)
    o_ref[...] = (_rms(os_[...], onorm_ref[...]) * (z * jax.nn.sigmoid(z))).astype(BF16)


def _gdn(p, p_small, conv_w, a_log, dt_bias, out_norm, batch, seq):
    row_blk = lambda col0: pl.BlockSpec((seq, LANES), lambda b, h, col0=col0: (b, col0 + h))
    conv_blk = lambda col0: pl.BlockSpec((DN_CONV, LANES), lambda b, h, col0=col0: (0, col0 + h))
    smem = pl.BlockSpec(memory_space=pltpu.SMEM)
    return pl.pallas_call(
        _gdn_kernel,
        grid=(batch, DN_HEADS),
        in_specs=[smem, smem,
                  row_blk(COL_Q), row_blk(COL_K), row_blk(COL_V), row_blk(COL_Z),
                  pl.BlockSpec((seq, LANES), lambda b, h: (b, 0)),
                  conv_blk(COL_Q), conv_blk(COL_K), conv_blk(COL_V),
                  pl.BlockSpec((1, LANES), lambda b, h: (0, 0))],
        out_specs=pl.BlockSpec((seq, LANES), lambda b, h: (b, h)),
        out_shape=jax.ShapeDtypeStruct((batch * seq, DN_W), BF16),
        scratch_shapes=([pltpu.VMEM((seq, LANES), F32) for _ in range(9)]
                        + [pltpu.VMEM((seq // DN_CHUNK, LANES), F32)]),
        compiler_params=_cparams(2),
        name="gdn",
    )(a_log, dt_bias, p, p, p, p, p_small, conv_w, conv_w, conv_w, out_norm)


def _gelu_tanh(x):
    return 0.5 * x * (1.0 + jnp.tanh(math.sqrt(2.0 / math.pi) * (x + 0.044715 * (x * x * x))))


def _compress_one(t_ref, pos_ref, w1_ref, w2_ref):
    t = t_ref[0].astype(F32)
    n = t.shape[0]
    y1 = _bdot(t + pos_ref[0], w1_ref[0])
    y2 = _bdot(t + pos_ref[1], w1_ref[1])
    hid = _gelu_tanh(y1 + pltpu.roll(y2, n - 1, axis=0))
    return _bdot(hid, w2_ref[...])


def _compress_kernel(tk_ref, tv_ref, pk_ref, pv_ref, w1k_ref, w2k_ref, w1v_ref, w2v_ref, kn_ref,
                     kc_ref, vc_ref):
    d = NSA_HEAD_DIM
    kc = _compress_one(tk_ref, pk_ref, w1k_ref, w2k_ref)
    kc_ref[0] = jnp.concatenate([_rms(kc[:, g * d:(g + 1) * d], kn_ref[...]) for g in range(NSA_GROUPS)],
                                axis=1)
    vc_ref[0] = _compress_one(tv_ref, pv_ref, w1v_ref, w2v_ref)


def _expand_compress_weights(pos, w1, w2):
    g, d, h = NSA_GROUPS, NSA_HEAD_DIM, CMP_HIDDEN
    eye = jnp.eye(g, dtype=F32)
    w1e = jnp.einsum('aldh,gk->algdkh', w1.reshape(2, CMP_STRIDE, d, h), eye)
    w1e = w1e.reshape(2, CMP_STRIDE * g * d, g * h).astype(BF16)
    w2e = jnp.einsum('hd,gk->ghkd', w2, eye).reshape(g * h, g * d).astype(BF16)
    pose = jnp.broadcast_to(pos.reshape(2, CMP_STRIDE, 1, d), (2, CMP_STRIDE, g, d))
    return pose.reshape(2, 1, CMP_STRIDE * g * d), w1e, w2e


def _compress(tk, tv, pos_k, w1k, w2k, pos_v, w1v, w2v, kn_cmp):
    batch, n, width = tk.shape
    tok = pl.BlockSpec((1, n, width), lambda b: (b, 0, 0))
    full = lambda a: pl.BlockSpec(a.shape, lambda b: (0,) * a.ndim, pipeline_mode=pl.Buffered(1))
    out = pl.BlockSpec((1, n, NSA_KV_W), lambda b: (b, 0, 0))
    return pl.pallas_call(
        _compress_kernel,
        grid=(batch,),
        in_specs=[tok, tok, full(pos_k), full(pos_v), full(w1k), full(w2k), full(w1v), full(w2v),
                  full(kn_cmp)],
        out_specs=[out, out],
        out_shape=[jax.ShapeDtypeStruct((batch, n, NSA_KV_W), F32)] * 2,
        compiler_params=_cparams(1),
        name="nsa_compress",
    )(tk, tv, pos_k, pos_v, w1k, w2k, w1v, w2v, kn_cmp)


def _nsa_kernel(q_ref, ks_ref, vs_ref, kw_ref, vw_ref, kc_ref, vc_ref, sm_ref, qn_ref, kns_ref,
                knw_ref, ovt_ref, gsel_ref, hsum_ref, o_ref, ksa, vsa, kwa, vwa, kca, vca, m_slc, acc_slc,
                m_win, acc_win, tile_used):
    it = pl.program_id(1)
    tile = ATT_TILE
    s_len = ks_ref.shape[0]
    d = NSA_HEAD_DIM
    rep = NSA_REP
    rows = rep * tile
    n_sel = s_len // SLC_LEN
    n_cend = kc_ref.shape[1]

    @pl.when(it == 0)
    def _():
        rowi = lax.broadcasted_iota(jnp.int32, (s_len, d), 0)
        xl = lax.broadcasted_iota(jnp.int32, (s_len, d), 1)
        in_tile = (rowi & (tile - 1)).astype(F32)
        tile_start = (rowi - (rowi & (tile - 1))).astype(F32)
        pos_cols = jnp.where((xl == X_SLOPE) | (xl == X_SLOPE + 1), in_tile,
                             jnp.where((xl == X_SLOPE + 2) | (xl == X_SLOPE + 3), tile_start, 0.0))
        blk_cols = jnp.where(xl == lax.shift_right_logical(rowi, int(math.log2(SLC_LEN))), 1.0, 0.0)
        ci = lax.broadcasted_iota(jnp.int32, (n_cend, d), 0)
        cl = lax.broadcasted_iota(jnp.int32, (n_cend, d), 1)
        cend_cols = jnp.where((cl == X_SLOPE) | (cl == X_SLOPE + 1), (CMP_STRIDE * ci).astype(F32), 0.0)
        ones_s = jnp.ones((s_len, LANES), F32)
        ones_c = jnp.ones((n_cend, LANES), F32)
        for g in range(NSA_GROUPS):
            sl = slice(g * d, (g + 1) * d)
            ksa[g] = jnp.concatenate([_rms(ks_ref[:, sl].astype(F32), kns_ref[...]), pos_cols + blk_cols],
                                     axis=1).astype(BF16)
            kwa[g] = jnp.concatenate([_rms(kw_ref[:, sl].astype(F32), knw_ref[...]), pos_cols],
                                     axis=1).astype(BF16)
            kca[g] = jnp.concatenate([kc_ref[0, :, sl], cend_cols], axis=1).astype(BF16)
            vs_g, vw_g = vs_ref[:, sl].astype(F32), vw_ref[:, sl].astype(F32)
            vsa[g] = jnp.concatenate([vs_g, vs_g, ones_s], axis=1).astype(BF16)
            vwa[g] = jnp.concatenate([vw_g, vw_g, ones_s], axis=1).astype(BF16)
            vca[g] = jnp.concatenate([vc_ref[0, :, sl], vc_ref[0, :, sl], ones_c], axis=1).astype(BF16)

    t0 = it * tile
    groups = range(NSA_GROUPS)
    row_r =lax.shift_right_logical(lax.broadcasted_iota(jnp.int32, (rows, d), 0),
                                    int(math.log2(tile)))
    tpos = t0 + (lax.broadcasted_iota(jnp.int32, (rows, LANES), 0) & (tile - 1))
    xl = lax.broadcasted_iota(jnp.int32, (rows, d), 1)
    col = lax.broadcasted_iota(jnp.int32, (rows, tile), 1)
    row_tt = lax.broadcasted_iota(jnp.int32, (rows, tile), 0) & (tile - 1)
    lower = col <= row_tt
    upper = col > row_tt
    cend = CMP_STRIDE * lax.broadcasted_iota(jnp.int32, (1, LANES), 1) + (CMP_LEN - 1)
    cmp_ok = tpos >= cend
    any_valid = (tpos >= CMP_LEN - 1).astype(F32)

    def attend(items):
        def scores(x, rb):
            r = slice(rb * ATT_ROW_BLOCK, (rb + 1) * ATT_ROW_BLOCK)
            s = _dot_nt(x[0][r], x[1])
            if x[3] is not None:
                s = jnp.where(x[3][r], s, NEG)
            return s, jnp.max(s, axis=-1, keepdims=True)

        def update(x, rb, s, s_max):
            r = slice(rb * ATT_ROW_BLOCK, (rb + 1) * ATT_ROW_BLOCK)
            g = x[6]
            m_old = x[4][g, r, :]
            m_new = jnp.maximum(m_old, jnp.broadcast_to(s_max, m_old.shape))
            alpha = jnp.exp2(m_old - m_new)
            p = jnp.exp2(s - jnp.concatenate([m_new] * (tile // LANES), axis=1))
            x[5][g, r, :] = jnp.concatenate([alpha, alpha], axis=1) * x[5][g, r, :] + _bdot(p, x[2])
            x[4][g, r, :] = m_new

        pending = None
        for x in items:
            for rb in range(rows // ATT_ROW_BLOCK):
                cur = (x, rb) + scores(x, rb)
                if pending is not None:
                    update(*pending)
                pending = cur
        update(*pending)

    log2e = math.log2(math.e)
    qf = q_ref[...].astype(F32)
    hw = hsum_ref.shape[0]
    sq_hi, sq_lo = _hi_lo(qf * qf)
    ssq = jnp.concatenate(
        [jnp.dot(sq_hi[:, j:j + hw], hsum_ref[...], preferred_element_type=F32)
         + jnp.dot(sq_lo[:, j:j + hw], hsum_ref[...], preferred_element_type=F32)
         for j in range(0, NSA_W, hw)], axis=1)
    qn = qf * lax.rsqrt(ssq * (1.0 / d) + EPS) * (qn_ref[...] * (log2e * d ** -0.5))
    q64, q_x, q_aug = [], [], []
    for g in groups:
        slopes = [log2e * 2.0 ** (-8.0 * (g * rep + r + 1) / NSA_HEADS) for r in range(rep)]
        sl = jnp.full((rows, d), slopes[rep - 1], F32)
        for r in range(rep - 2, -1, -1):
            sl = jnp.where(row_r == r, slopes[r], sl)
        sl_hi = sl.astype(BF16).astype(F32)
        q_x.append(jnp.where((xl == X_SLOPE) | (xl == X_SLOPE + 2), sl_hi,
                             jnp.where((xl == X_SLOPE + 1) | (xl == X_SLOPE + 3), sl - sl_hi, 0.0)))
        q64.append(jnp.concatenate(
            [qn[:, (g * rep + r) * d:(g * rep + r + 1) * d] for r in range(rep)], axis=0))
        q_aug.append(jnp.concatenate([q64[g], q_x[g]], axis=1).astype(BF16))

    s_c = [jnp.where(cmp_ok, _dot_nt(q_aug[g], kca[g]), NEG) for g in groups]
    e_c = [jnp.exp2(s - jnp.max(s, axis=-1, keepdims=True)) for s in s_c]
    pv_c = [_bdot(e_c[g], vca[g]) for g in groups]
    w_c = [any_valid / pv[:, LANES:] for pv in pv_c]
    p_c = [e_c[g] * w_c[g] for g in groups]
    o_cmp = [pv_c[g][:, :LANES] * w_c[g] for g in groups]
    p_sum = [sum(p[r * tile:(r + 1) * tile] for r in range(rep)) for p in p_c]

    jj = lax.broadcasted_iota(jnp.int32, (X_SLOPE, tile), 0)
    blk_t = lax.shift_right_logical(t0 + lax.broadcasted_iota(jnp.int32, (1, tile), 1), int(math.log2(SLC_LEN)))
    imp = [sum(_dot_nt(ovt_ref[...], t) for t in _split3(ps))[:X_SLOPE] for ps in p_sum]
    imp = [jnp.where(jj <= blk_t, x, NEG) for x in imp]
    imp = [jnp.where((jj == blk_t) | (jj == 0), FORCE, x) for x in imp]
    imp = [jnp.where(jj < n_sel, x, -3e38) for x in imp]
    rank = [jnp.zeros((X_SLOPE, tile), F32) for _ in groups]
    for j in range(n_sel):
        for g in groups:
            row = imp[g][j:j + 1, :]
            ge = jnp.where(row >= imp[g], 1.0, 0.0)
            gt = jnp.where(row > imp[g], 1.0, 0.0)
            rank[g] = rank[g] + jnp.where(jj > j, ge, gt)
    k_sel = float(min(SLC_TOPK, n_sel))
    bias_t = [jnp.where((rk < k_sel) | (jj >= n_sel), 0.0, NEG) for rk in rank]
    zero_rows = jnp.zeros((LANES - X_SLOPE, tile), F32)
    sel_bias = [jnp.concatenate([bt, zero_rows], axis=0).T[:, :d] for bt in bias_t]
    q_sel = [jnp.concatenate([q64[g], q_x[g] + jnp.concatenate([sel_bias[g]] * rep, axis=0)],
                             axis=1).astype(BF16) for g in groups]

    for g in groups:
        m_slc[g] = jnp.full((rows, LANES), -jnp.inf, F32)
        m_win[g] = jnp.full((rows, LANES), -jnp.inf, F32)
        acc_slc[g] = jnp.zeros((rows, 2 * LANES), F32)
        acc_win[g] = jnp.zeros((rows, 2 * LANES), F32)

    r_diag = pl.ds(pl.multiple_of(t0, tile), tile)
    attend([(q_sel[g], ksa[g, r_diag, :], vsa[g, r_diag, :], lower, m_slc, acc_slc, g) for g in groups])
    attend([(q_aug[g], kwa[g, r_diag, :], vwa[g, r_diag, :], lower, m_win, acc_win, g) for g in groups])

    @pl.when(it > 0)
    def _():
        r_prev = pl.ds(pl.multiple_of(t0 - tile, tile), tile)
        attend([(q_aug[g], kwa[g, r_prev, :], vwa[g, r_prev, :], upper, m_win, acc_win, g) for g in groups])

    picked = functools.reduce(jnp.maximum, [jnp.where(rk < k_sel, 1.0, 0.0) for rk in rank])
    blocks_per_tile = tile // SLC_LEN
    for kt in range(s_len // tile):
        tile_used[kt] = jnp.max(picked[kt * blocks_per_tile:(kt + 1) * blocks_per_tile, :]).astype(jnp.int32)

    def slc_tile(kt, carry):
        @pl.when(tile_used[kt] > 0)
        def _():
            r = pl.ds(pl.multiple_of(kt * tile, tile), tile)
            attend([(q_sel[g], ksa[g, r, :], vsa[g, r, :], None, m_slc, acc_slc, g) for g in groups])

        return carry

    lax.fori_loop(0, it, slc_tile, 0)

    g_hi, g_lo = _hi_lo(jax.nn.sigmoid(sm_ref[...]))
    gate_w = (jnp.dot(g_hi, gsel_ref[...], preferred_element_type=F32)
              + jnp.dot(g_lo, gsel_ref[...], preferred_element_type=F32))
    low_half = lax.broadcasted_iota(jnp.int32, (tile, LANES), 1) < d
    for g in groups:
        o_slc = acc_slc[g][:, :LANES] / acc_slc[g][:, LANES:]
        o_win = acc_win[g][:, :LANES] / acc_win[g][:, LANES:]
        for pr in range(rep // 2):
            pair = g * (rep // 2) + pr
            r_even = slice(2 * pr * tile, (2 * pr + 1) * tile)
            r_odd = slice((2 * pr + 1) * tile, (2 * pr + 2) * tile)
            merged = 0.0
            for br, o in enumerate((o_cmp[g], o_slc, o_win)):
                c0 = (pair * 3 + br) * LANES
                merged = merged + gate_w[:, c0:c0 + LANES] * jnp.where(low_half, o[r_even], o[r_odd])
            o_ref[:, pair * LANES:(pair + 1) * LANES] = merged.astype(BF16)


def _gate_spread():
    n_pairs = NSA_HEADS // 2
    c = np.arange(LANES)[:, None]
    col = np.arange(n_pairs * 3 * LANES)[None, :]
    pair, br, n = col // (3 * LANES), (col // LANES) % 3, col % LANES
    head = 2 * pair + (n >= NSA_HEAD_DIM)
    return jnp.asarray((c == SMALL_GATE + 3 * head + br).astype(np.float32), BF16)


def _head_sum(width=2 * LANES):
    i = np.arange(width)
    return jnp.asarray((i[:, None] // NSA_HEAD_DIM == i[None, :] // NSA_HEAD_DIM).astype(np.float32), BF16)


def _nsa(p, p_small, kc, vc, q_norm, kn_slc, kn_win, ovt, batch, seq):
    gsel, hsum = _gate_spread(), _head_sum()
    q_gain = jnp.tile(q_norm, (1, NSA_HEADS))
    tile = ATT_TILE
    nt = seq // tile
    rows = NSA_REP * tile
    kv = lambda col: pl.BlockSpec((seq, NSA_KV_W), lambda b, i, col=col: (b, col // 2),
                                  pipeline_mode=pl.Buffered(1))
    full = lambda a: pl.BlockSpec(a.shape, lambda b, i: (0,) * a.ndim)
    cmp_blk = pl.BlockSpec((1,) + kc.shape[1:], lambda b, i: (b, 0, 0))
    aug = lambda n: pltpu.VMEM((NSA_GROUPS, n, LANES), BF16)
    val = lambda n: pltpu.VMEM((NSA_GROUPS, n, 2 * LANES), BF16)
    stat = pltpu.VMEM((NSA_GROUPS, rows, LANES), F32)
    acc = pltpu.VMEM((NSA_GROUPS, rows, 2 * LANES), F32)
    return pl.pallas_call(
        _nsa_kernel,
        grid=(batch, nt),
        in_specs=[pl.BlockSpec((tile, NSA_W), lambda b, i: (b * nt + i, COL_NSQ // 8)),
                  kv(COL_KS), kv(COL_VS), kv(COL_KW), kv(COL_VW),
                  cmp_blk, cmp_blk,
                  pl.BlockSpec((tile, LANES), lambda b, i: (b * nt + i, 0)),
                  full(q_gain), full(kn_slc), full(kn_win), full(ovt), full(gsel), full(hsum)],
        out_specs=pl.BlockSpec((tile, NSA_W), lambda b, i: (b * nt + i, 0)),
        out_shape=jax.ShapeDtypeStruct((batch * seq, NSA_W), BF16),
        scratch_shapes=[aug(seq), val(seq), aug(seq), val(seq), aug(kc.shape[1]), val(kc.shape[1]),
                        stat, acc, stat, acc, pltpu.SMEM((nt,), jnp.int32)],
        compiler_params=_cparams(2),
        name="nsa_attention",
    )(p, p, p, p, p, kc, vc, p_small, q_gain, kn_slc, kn_win, ovt, gsel, hsum)


def _mix_kernel(x_ref, odn_ref, ons_ref, gdn_ref, gns_ref, wdn_ref, wns_ref, wout_ref, o_ref):
    y_dn = jnp.dot(odn_ref[...], wdn_ref[...], preferred_element_type=F32)
    y_ns = jnp.dot(ons_ref[...], wns_ref[...], preferred_element_type=F32)
    mix = (jax.nn.sigmoid(gdn_ref[...].astype(F32)) * y_dn
           + jax.nn.sigmoid(gns_ref[...].astype(F32)) * y_ns)
    o_ref[...] = x_ref[...] + jnp.dot(mix.astype(BF16), wout_ref[...], preferred_element_type=F32)


def _mix(x2, o_dn, o_ns, p, w_dn, w_ns, w_out, tm=512):
    t = x2.shape[0]
    row = lambda col: pl.BlockSpec((tm, D_MODEL), lambda i, col=col: (i, col))
    wfull = pl.BlockSpec((D_MODEL, D_MODEL), lambda i: (0, 0))
    return pl.pallas_call(
        _mix_kernel,
        grid=(t // tm,),
        in_specs=[row(0), row(0), row(0), row(COL_GDN // 8), row(COL_GNS // 8), wfull, wfull, wfull],
        out_specs=row(0),
        out_shape=jax.ShapeDtypeStruct((t, D_MODEL), F32),
        compiler_params=_cparams(1),
        name="mix_out",
    )(x2, o_dn, o_ns, p, p, w_dn, w_ns, w_out)


def _ffn_kernel(x_ref, g_ref, wup_ref, wdown_ref, o_ref, h_ref, acc_ref):
    f = pl.program_id(1)

    @pl.when(f == 0)
    def _():
        h_ref[...] = _rms(x_ref[...], g_ref[...]).astype(BF16)
        acc_ref[...] = jnp.zeros_like(acc_ref)

    u = jnp.dot(h_ref[...], wup_ref[...], preferred_element_type=F32)
    u = jnp.square(jnp.maximum(u, 0.0)).astype(BF16)
    acc_ref[...] += jnp.dot(u, wdown_ref[...], preferred_element_type=F32)

    @pl.when(f == pl.num_programs(1) - 1)
    def _():
        o_ref[...] = x_ref[...] + acc_ref[...]


def _ffn(x2, gain, w_up, w_down, tm=1024, tf=2048):
    t = x2.shape[0]
    return pl.pallas_call(
        _ffn_kernel,
        grid=(t // tm, D_FF // tf),
        in_specs=[pl.BlockSpec((tm, D_MODEL), lambda i, f: (i, 0)),
                  pl.BlockSpec((1, D_MODEL), lambda i, f: (0, 0)),
                  pl.BlockSpec((D_MODEL, tf), lambda i, f: (0, f)),
                  pl.BlockSpec((tf, D_MODEL), lambda i, f: (f, 0))],
        out_specs=pl.BlockSpec((tm, D_MODEL), lambda i, f: (i, 0)),
        out_shape=jax.ShapeDtypeStruct((t, D_MODEL), F32),
        scratch_shapes=[pltpu.VMEM((tm, D_MODEL), BF16), pltpu.VMEM((tm, D_MODEL), F32)],
        compiler_params=_cparams(2),
        name="ffn",
    )(x2, gain, w_up, w_down)


def _overlap_matrix_t(seq):
    n_cmp = (seq - CMP_LEN) // CMP_STRIDE + 1
    n_sel = seq // SLC_LEN
    c0 = np.arange(n_cmp)[None, :] * CMP_STRIDE
    j0 = np.arange(n_sel)[:, None] * SLC_LEN
    ov = np.clip(np.minimum(c0 + CMP_LEN, j0 + SLC_LEN) - np.maximum(c0, j0), 0, None) / CMP_LEN
    out = np.zeros((LANES, LANES), np.float32)
    out[:n_sel, :n_cmp] = ov
    return jnp.asarray(out, BF16)


def _regroup_w_in(w):
    o_b = 4 * DN_W
    o_q = o_b + 2 * DN_HEADS
    o_kv = o_q + NSA_W
    o_gate = o_kv + 6 * NSA_KV_W
    o_gdn = o_gate + 3 * NSA_HEADS
    pad = jnp.zeros((w.shape[0], P_WIDTH - w.shape[1]), w.dtype)
    return jnp.concatenate(
        [w[:, :o_b], w[:, o_q:o_kv], w[:, o_gdn:], w[:, o_kv:o_gate], w[:, o_b:o_q],
         w[:, o_gate:o_gdn], pad], axis=1).astype(BF16)


def _tokens16(p, col, batch, seq):
    t = p[:, col * LANES:col * LANES + NSA_KV_W]
    return t.reshape(batch, seq // CMP_STRIDE, CMP_STRIDE * NSA_KV_W)


def kernel(x, norm_mix, w_in, dn_conv, dn_a_log, dn_dt_bias, dn_out_norm, nsa_q_norm, nsa_k_norm_cmp,
           nsa_k_norm_slc, nsa_k_norm_win, cmp_pos_k, cmp_w1_k, cmp_w2_k, cmp_pos_v, cmp_w1_v, cmp_w2_v,
           w_proj_dn, w_proj_nsa, w_out, norm_mlp, w_up, w_down):
    batch, seq, _ = x.shape
    assert seq // CMP_STRIDE == LANES and seq % ATT_TILE == 0 and seq // SLC_LEN <= X_SLOPE
    ovt = _overlap_matrix_t(seq)
    x2 = x.reshape(batch * seq, D_MODEL)
    for l in range(w_in.shape[0]):
        p, p_small = _in_proj(x2, norm_mix[l][None], _regroup_w_in(w_in[l]))
        o_dn = _gdn(p, p_small, dn_conv[l], dn_a_log[l][None], dn_dt_bias[l][None], dn_out_norm[l][None],
                    batch, seq)
        kc, vc = _compress(_tokens16(p, COL_KC, batch, seq), _tokens16(p, COL_VC, batch, seq),
                           *_expand_compress_weights(cmp_pos_k[l], cmp_w1_k[l], cmp_w2_k[l]),
                           *_expand_compress_weights(cmp_pos_v[l], cmp_w1_v[l], cmp_w2_v[l]),
                           nsa_k_norm_cmp[l][None])
        o_ns = _nsa(p, p_small, kc, vc, nsa_q_norm[l][None], nsa_k_norm_slc[l][None], nsa_k_norm_win[l][None],
                    ovt, batch, seq)
        x2 = _mix(x2, o_dn, o_ns, p, w_proj_dn[l].astype(BF16), w_proj_nsa[l].astype(BF16),
                  w_out[l].astype(BF16))
        x2 = _ffn(x2, norm_mlp[l][None], w_up[l].astype(BF16), w_down[l].astype(BF16))
    return x2.reshape(batch, seq, D_MODEL)
```
